```python
import math
import jax, jax.numpy as jnp
from jax import lax
import numpy as np

D_MODEL = 1024
BATCH = 8
SEQ = 2048
DEPTH = 1
DEC_BATCH = 128
DEC_SEQ = 4
PAST_LEN = 16384
PAGE_SIZE = 128

N_META = 16
D_RNN = D_MODEL
N_RNN_HEADS = 16
RNN_HEAD_DIM = D_RNN // N_RNN_HEADS
RNN_CONV_W = 4
LRU_C = 8.0
D_CONV = D_MODEL
CONF_CONV_W = 31
N_GROUPS = 4
EXPERTS_PER_GROUP = 8
N_EXPERTS = N_GROUPS * EXPERTS_PER_GROUP
TOP_K_IN_GROUP = 2
D_EXPERT = D_MODEL // 4
EPS = 1e-6
SPLITS = (D_RNN, 2 * D_RNN, 2 * D_RNN + D_CONV, 2 * D_RNN + 2 * D_CONV, 2 * D_RNN + 2 * D_CONV + D_MODEL)
D_IN_TOTAL = 2 * D_RNN + 2 * D_CONV + 2 * D_MODEL

kernel_name = 'hybrid_rglru_conformer_hmoe_step'


def _rmsnorm(x, g):
    xf = x.astype(jnp.float32)
    y = xf * lax.rsqrt(jnp.mean(xf * xf, axis=-1, keepdims=True) + EPS)
    return (y * g.astype(jnp.float32)).astype(x.dtype)


def _layernorm(x, g, b):
    xf = x.astype(jnp.float32)
    mu = jnp.mean(xf, axis=-1, keepdims=True)
    var = jnp.mean(jnp.square(xf - mu), axis=-1, keepdims=True)
    y = (xf - mu) * lax.rsqrt(var + EPS)
    return (y * g.astype(jnp.float32) + b.astype(jnp.float32)).astype(x.dtype)


def _causal_dwconv(x, buf, w, b):
    width, chans = w.shape
    xp = jnp.concatenate([buf.astype(x.dtype), x], axis=1)
    y = lax.conv_general_dilated(xp, w.astype(x.dtype)[:, None, :], window_strides=(1,), padding='VALID',
                                 dimension_numbers=('NWC', 'WIO', 'NWC'), feature_group_count=chans)
    return y + b.astype(x.dtype), xp[:, -(width - 1):]


def _lru_scan(a, bseq, h0):
    def combine(left, right):
        a1, b1 = left
        a2, b2 = right
        return a1 * a2, a2 * b1 + b2
    a_cum, h_zero = lax.associative_scan(combine, (a, bseq), axis=1)
    h = h_zero + a_cum * h0.astype(jnp.float32)[:, None, :]
    return h, h[:, -1]


def _mixer(xn, h0, c4, c31, w_in, rnn_conv_w, rnn_conv_b, w_rg_a, b_rg_a, w_rg_i, b_rg_i, rg_lambda,
           w_rnn_proj, conf_conv_w, conf_conv_b, conf_ln_g, conf_ln_b, w_conv_proj, w_out):
    bsz, t = xn.shape[0], xn.shape[1]
    proj = xn @ w_in
    u_x, u_y, c_v, c_g, g_rnn, g_conv = jnp.split(proj, SPLITS, axis=-1)
    xc, new_c4 = _causal_dwconv(u_x, c4, rnn_conv_w, rnn_conv_b)
    xh = xc.reshape(bsz, t, N_RNN_HEADS, RNN_HEAD_DIM)
    r = jax.nn.sigmoid((jnp.einsum('bthi,hij->bthj', xh, w_rg_a).reshape(bsz, t, D_RNN) + b_rg_a).astype(jnp.float32))
    i = jax.nn.sigmoid((jnp.einsum('bthi,hij->bthj', xh, w_rg_i).reshape(bsz, t, D_RNN) + b_rg_i).astype(jnp.float32))
    log_a = -LRU_C * r * jax.nn.softplus(-rg_lambda.astype(jnp.float32))
    a = jnp.exp(log_a)
    bseq = jnp.sqrt(-jnp.expm1(2.0 * log_a)) * (i * xc.astype(jnp.float32))
    h, h_last = _lru_scan(a, bseq, h0)
    rnn_out = (h.astype(xn.dtype) * jax.nn.gelu(u_y)) @ w_rnn_proj
    glu = c_v * jax.nn.sigmoid(c_g)
    cc, new_c31 = _causal_dwconv(glu, c31, conf_conv_w, conf_conv_b)
    conv_out = jax.nn.silu(_layernorm(cc, conf_ln_g, conf_ln_b)) @ w_conv_proj
    merged = jax.nn.sigmoid(g_rnn) * rnn_out + jax.nn.sigmoid(g_conv) * conv_out
    return merged @ w_out, h_last, new_c4, new_c31


def _hier_moe(xn, w_group, b_group, w_erouter, b_erouter, w_exp_gate, w_exp_up, w_exp_down):
    xf = xn.reshape(-1, D_MODEL)
    n = xf.shape[0]
    gl = (xf @ w_group).astype(jnp.float32) + b_group.astype(jnp.float32)
    gp = jax.nn.softmax(gl, axis=-1)
    _, gidx = lax.top_k(gl, 1)
    gidx = gidx[:, 0]
    g_gate = jnp.take_along_axis(gp, gidx[:, None], axis=1)[:, 0]
    el = (xf @ w_erouter).astype(jnp.float32).reshape(n, N_GROUPS, EXPERTS_PER_GROUP) + b_erouter.astype(jnp.float32)
    el_sel = jnp.take_along_axis(el, gidx[:, None, None], axis=1)[:, 0]
    ev, eidx = lax.top_k(el_sel, TOP_K_IN_GROUP)
    ew = jax.nn.softmax(ev, axis=-1) * g_gate[:, None]
    flat_idx = gidx[:, None] * EXPERTS_PER_GROUP + eidx
    combine = jnp.sum(jax.nn.one_hot(flat_idx, N_EXPERTS, dtype=jnp.float32) * ew[..., None], axis=1)
    combine = combine.astype(xf.dtype)
    y = jnp.zeros_like(xf)
    for e in range(N_EXPERTS):
        he = jax.nn.silu(xf @ w_exp_gate[e]) * (xf @ w_exp_up[e])
        y = y + combine[:, e:e + 1] * (he @ w_exp_down[e])
    return y.reshape(xn.shape)


def _trunk(x, h0, c4, c31, norm1_g, w_in, rnn_conv_w, rnn_conv_b, w_rg_a, b_rg_a, w_rg_i, b_rg_i, rg_lambda,
           w_rnn_proj, conf_conv_w, conf_conv_b, conf_ln_g, conf_ln_b, w_conv_proj, w_out, norm2_g,
           w_group, b_group, w_erouter, b_erouter, w_exp_gate, w_exp_up, w_exp_down, final_norm_g):
    hs, c4s, c31s = [], [], []
    for l in range(DEPTH):
        mix, h_l, c4_l, c31_l = _mixer(_rmsnorm(x, norm1_g[l]), h0[l], c4[l], c31[l], w_in[l], rnn_conv_w[l],
                                       rnn_conv_b[l], w_rg_a[l], b_rg_a[l], w_rg_i[l], b_rg_i[l], rg_lambda[l],
                                       w_rnn_proj[l], conf_conv_w[l], conf_conv_b[l], conf_ln_g[l], conf_ln_b[l],
                                       w_conv_proj[l], w_out[l])
        x = x + mix
        x = x + _hier_moe(_rmsnorm(x, norm2_g[l]), w_group[l], b_group[l], w_erouter[l], b_erouter[l],
                          w_exp_gate[l], w_exp_up[l], w_exp_down[l])
        hs.append(h_l)
        c4s.append(c4_l)
        c31s.append(c31_l)
    return _rmsnorm(x, final_norm_g), jnp.stack(hs), jnp.stack(c4s), jnp.stack(c31s)


def setup_inputs(seed: int = 0) -> dict:
    key = jax.random.key(seed)
    ks = jax.random.split(key, 40)
    f32 = jnp.float32

    def nrm(k, shape, scale):
        return jax.random.normal(k, shape, f32) * scale

    u = jax.random.uniform(ks[12], (DEPTH, D_RNN), f32, minval=0.9, maxval=0.999)
    s = u ** (1.0 / LRU_C)
    rg_lambda = jnp.log(s) - jnp.log1p(-s)
    return {
        'x_prompt': nrm(ks[0], (BATCH, SEQ, D_MODEL), 1.0),
        'x_sample': nrm(ks[1], (DEC_BATCH, DEC_SEQ, D_MODEL), 1.0),
        'state_rglru_h': nrm(ks[2], (DEPTH, DEC_BATCH, D_RNN), 0.5),
        'state_rglru_conv': nrm(ks[3], (DEPTH, DEC_BATCH, RNN_CONV_W - 1, D_RNN), 1.0),
        'state_conformer_conv': nrm(ks[4], (DEPTH, DEC_BATCH, CONF_CONV_W - 1, D_CONV), 0.5),
        'meta_tokens': nrm(ks[5], (N_META, D_MODEL), 1.0),
        'norm1_g': 1.0 + nrm(ks[6], (DEPTH, D_MODEL), 0.02),
        'w_in': nrm(ks[7], (DEPTH, D_MODEL, D_IN_TOTAL), D_MODEL ** -0.5),
        'rnn_conv_w': nrm(ks[8], (DEPTH, RNN_CONV_W, D_RNN), RNN_CONV_W ** -0.5),
        'rnn_conv_b': nrm(ks[9], (DEPTH, D_RNN), 0.02),
        'w_rg_a': nrm(ks[10], (DEPTH, N_RNN_HEADS, RNN_HEAD_DIM, RNN_HEAD_DIM), RNN_HEAD_DIM ** -0.5),
        'b_rg_a': nrm(ks[11], (DEPTH, D_RNN), 0.02),
        'w_rg_i': nrm(ks[13], (DEPTH, N_RNN_HEADS, RNN_HEAD_DIM, RNN_HEAD_DIM), RNN_HEAD_DIM ** -0.5),
        'b_rg_i': nrm(ks[14], (DEPTH, D_RNN), 0.02),
        'rg_lambda': rg_lambda,
        'w_rnn_proj': nrm(ks[15], (DEPTH, D_RNN, D_MODEL), D_RNN ** -0.5),
        'conf_conv_w': nrm(ks[16], (DEPTH, CONF_CONV_W, D_CONV), CONF_CONV_W ** -0.5),
        'conf_conv_b': nrm(ks[17], (DEPTH, D_CONV), 0.02),
        'conf_ln_g': 1.0 + nrm(ks[18], (DEPTH, D_CONV), 0.02),
        'conf_ln_b': nrm(ks[19], (DEPTH, D_CONV), 0.02),
        'w_conv_proj': nrm(ks[20], (DEPTH, D_CONV, D_MODEL), D_CONV ** -0.5),
        'w_out': nrm(ks[21], (DEPTH, D_MODEL, D_MODEL), D_MODEL ** -0.5),
        'norm2_g': 1.0 + nrm(ks[22], (DEPTH, D_MODEL), 0.02),
        'w_group': nrm(ks[23], (DEPTH, D_MODEL, N_GROUPS), D_MODEL ** -0.5),
        'b_group': nrm(ks[24], (DEPTH, N_GROUPS), 0.01),
        'w_erouter': nrm(ks[25], (DEPTH, D_MODEL, N_GROUPS * EXPERTS_PER_GROUP), D_MODEL ** -0.5),
        'b_erouter': nrm(ks[26], (DEPTH, N_GROUPS, EXPERTS_PER_GROUP), 0.01),
        'w_exp_gate': nrm(ks[27], (DEPTH, N_EXPERTS, D_MODEL, D_EXPERT), D_MODEL ** -0.5),
        'w_exp_up': nrm(ks[28], (DEPTH, N_EXPERTS, D_MODEL, D_EXPERT), D_MODEL ** -0.5),
        'w_exp_down': nrm(ks[29], (DEPTH, N_EXPERTS, D_EXPERT, D_MODEL), D_EXPERT ** -0.5),
        'final_norm_g': 1.0 + nrm(ks[30], (D_MODEL,), 0.02),
    }


def reference(x_prompt, x_sample, state_rglru_h, state_rglru_conv, state_conformer_conv, meta_tokens,
              norm1_g, w_in, rnn_conv_w, rnn_conv_b, w_rg_a, b_rg_a, w_rg_i, b_rg_i, rg_lambda, w_rnn_proj,
              conf_conv_w, conf_conv_b, conf_ln_g, conf_ln_b, w_conv_proj, w_out, norm2_g, w_group, b_group,
              w_erouter, b_erouter, w_exp_gate, w_exp_up, w_exp_down, final_norm_g):
    weights = (norm1_g, w_in, rnn_conv_w, rnn_conv_b, w_rg_a, b_rg_a, w_rg_i, b_rg_i, rg_lambda, w_rnn_proj,
               conf_conv_w, conf_conv_b, conf_ln_g, conf_ln_b, w_conv_proj, w_out, norm2_g, w_group, b_group,
               w_erouter, b_erouter, w_exp_gate, w_exp_up, w_exp_down, final_norm_g)
    bsz = x_prompt.shape[0]
    meta = jnp.broadcast_to(meta_tokens.astype(x_prompt.dtype)[None], (bsz, N_META, D_MODEL))
    xp = jnp.concatenate([meta, x_prompt], axis=1)
    h0 = jnp.zeros((DEPTH, bsz, D_RNN), jnp.float32)
    c4_0 = jnp.zeros((DEPTH, bsz, RNN_CONV_W - 1, D_RNN), x_prompt.dtype)
    c31_0 = jnp.zeros((DEPTH, bsz, CONF_CONV_W - 1, D_CONV), x_prompt.dtype)
    yp, h_p, c4_p, c31_p = _trunk(xp, h0, c4_0, c31_0, *weights)
    ys, h_s, c4_s, c31_s = _trunk(x_sample, state_rglru_h, state_rglru_conv, state_conformer_conv, *weights)
    return (yp[:, N_META:], ys, h_p, c4_p, c31_p, h_s, c4_s, c31_s)
```

```python
import functools

import jax
import jax.numpy as jnp
from jax import lax
from jax.experimental import pallas as pl
from jax.experimental.pallas import tpu as pltpu

D = 1024
N_META = 16
N_HEADS = 16
HEAD_DIM = D // N_HEADS
RNN_CONV_W = 4
CONF_CONV_W = 31
LRU_C = 8.0
N_GROUPS = 4
EXPERTS_PER_GROUP = 8
N_EXPERTS = N_GROUPS * EXPERTS_PER_GROUP
D_EXPERT = D // 4
EPS = 1e-6

SUBLANES = 8
LANES = 128
MXU_DIM = 256
HEADS_PER_BLOCK = MXU_DIM // HEAD_DIM
N_GATE_BLOCKS = D // MXU_DIM
VMEM_LIMIT = 56 * 1024 * 1024

F32 = jnp.float32
BF16 = jnp.bfloat16

(V_NORM1, V_CONV4_B, V_BA, V_BI, V_LAMBDA, V_CONV31_B, V_LN_G, V_LN_B, V_NORM2) = range(9)
N_VEC_ROWS = 16


def _dot(a, b):
    return jnp.dot(a, b, preferred_element_type=F32)


def _slab_loop(n, body):
    def wrapped(s, carry):
        body(pl.multiple_of(s * SUBLANES, SUBLANES))
        return carry
    lax.fori_loop(0, n, wrapped, 0)


def _mixer_kernel(x_ref, h0_ref, c4_ref, c31_ref, vec_ref, cw4_ref, cw31_ref,
                  w_in_ref, w_gate_ref, w_rnn_ref, w_conv_ref, w_out_ref, wr_hi_ref, wr_lo_ref, br_ref,
                  x1_ref, xn2_ref, logit_ref, h_out_ref, c4_out_ref, c31_out_ref,
                  xn_s, ux_s, uy_s, cv_s, cg_s, grnn_s, gconv_s, glu_s, xc_s, ga_s, gi_s, hy_s, cn_s, h_s,
                  *, t_tile, b_blk):
    i = pl.program_id(1)
    rows = t_tile * b_blk
    n_slab = rows // SUBLANES
    p4 = (RNN_CONV_W - 1) * b_blk
    p31 = (CONF_CONV_W - 1) * b_blk

    @pl.when(i == 0)
    def _():
        h_s[...] = h0_ref[...]
        ux_s[0:p4, :] = c4_ref[...]
        glu_s[0:p31, :] = c31_ref[...]

    def sl(r0):
        return pl.ds(r0, SUBLANES)

    def vec(row):
        return vec_ref[row:row + 1, :]

    def norm1(r0):
        x = x_ref[sl(r0), :]
        ms = jnp.mean(x * x, axis=-1, keepdims=True)
        xn_s[sl(r0), :] = x * lax.rsqrt(ms + EPS) * vec(V_NORM1)
    _slab_loop(n_slab, norm1)

    xb = xn_s[...].astype(BF16)
    ux_s[p4:p4 + rows, :] = _dot(xb, w_in_ref[:, 0 * D:1 * D])
    uy_s[...] = _dot(xb, w_in_ref[:, 1 * D:2 * D])
    cv_s[...] = _dot(xb, w_in_ref[:, 2 * D:3 * D])
    cg_s[...] = _dot(xb, w_in_ref[:, 3 * D:4 * D])
    grnn_s[...] = _dot(xb, w_in_ref[:, 4 * D:5 * D])
    gconv_s[...] = _dot(xb, w_in_ref[:, 5 * D:6 * D])

    def conv4_glu(r0):
        acc = ux_s[sl(r0), :] * cw4_ref[0:1, :]
        for k in range(1, RNN_CONV_W):
            acc = acc + ux_s[sl(r0 + k * b_blk), :] * cw4_ref[k:k + 1, :]
        xc_s[sl(r0), :] = acc + vec(V_CONV4_B)
        glu_s[sl(p31 + r0), :] = cv_s[sl(r0), :] * jax.nn.sigmoid(cg_s[sl(r0), :])
    _slab_loop(n_slab, conv4_glu)

    xcb = xc_s[...].astype(BF16)
    for blk in range(N_GATE_BLOCKS):
        cols = slice(blk * MXU_DIM, (blk + 1) * MXU_DIM)
        g = _dot(xcb[:, cols], w_gate_ref[blk])
        ga_s[:, cols] = g[:, :MXU_DIM]
        gi_s[:, cols] = g[:, MXU_DIM:]

    c_lam = -LRU_C * jax.nn.softplus(-vec(V_LAMBDA))

    def scan_batch_slab(jb, carry):
        b0 = pl.multiple_of(jb * SUBLANES, SUBLANES)

        def step(t, h):
            r0 = pl.multiple_of(t * b_blk + b0, SUBLANES)
            xc = xc_s[sl(r0), :]
            r = jax.nn.sigmoid(ga_s[sl(r0), :] + vec(V_BA))
            ig = jax.nn.sigmoid(gi_s[sl(r0), :] + vec(V_BI))
            log_a = c_lam * r
            a = jnp.exp(log_a)
            mult = jnp.sqrt(-jnp.tanh(log_a) * (a * a + 1.0))
            h = a * h + mult * (ig * xc)
            hy_s[sl(r0), :] = h * jax.nn.gelu(uy_s[sl(r0), :])
            return h

        h_s[sl(b0), :] = lax.fori_loop(0, t_tile, step, h_s[sl(b0), :])
        return carry
    lax.fori_loop(0, b_blk // SUBLANES, scan_batch_slab, 0)

    def conv31_ln(r0):
        acc = glu_s[sl(r0), :] * cw31_ref[0:1, :]
        for k in range(1, CONF_CONV_W):
            acc = acc + glu_s[sl(r0 + k * b_blk), :] * cw31_ref[k:k + 1, :]
        cc = acc + vec(V_CONV31_B)
        mu = jnp.mean(cc, axis=-1, keepdims=True)
        cen = cc - mu
        var = jnp.mean(cen * cen, axis=-1, keepdims=True)
        y = cen * lax.rsqrt(var + EPS) * vec(V_LN_G) + vec(V_LN_B)
        cn_s[sl(r0), :] = y * jax.nn.sigmoid(y)
    _slab_loop(n_slab, conv31_ln)

    ga_s[...] = _dot(hy_s[...].astype(BF16), w_rnn_ref[...])
    gi_s[...] = _dot(cn_s[...].astype(BF16), w_conv_ref[...])

    def merge(r0):
        hy_s[sl(r0), :] = (jax.nn.sigmoid(grnn_s[sl(r0), :]) * ga_s[sl(r0), :]
                           + jax.nn.sigmoid(gconv_s[sl(r0), :]) * gi_s[sl(r0), :])
    _slab_loop(n_slab, merge)
    cn_s[...] = _dot(hy_s[...].astype(BF16), w_out_ref[...])

    def residual_norm2(r0):
        x1 = x_ref[sl(r0), :] + cn_s[sl(r0), :]
        x1_ref[sl(r0), :] = x1
        ms = jnp.mean(x1 * x1, axis=-1, keepdims=True)
        xn_s[sl(r0), :] = x1 * lax.rsqrt(ms + EPS) * vec(V_NORM2)
    _slab_loop(n_slab, residual_norm2)

    xn2 = xn_s[...]
    hi = xn2.astype(BF16)
    lo = (xn2 - hi.astype(F32)).astype(BF16)
    xn2_ref[...] = hi
    logit_ref[...] = (_dot(hi, wr_hi_ref[...]) + _dot(lo, wr_hi_ref[...]) + _dot(hi, wr_lo_ref[...])
                      + br_ref[...])

    for k in range(RNN_CONV_W - 1):
        ux_s[k * b_blk:(k + 1) * b_blk, :] = ux_s[(t_tile + k) * b_blk:(t_tile + k + 1) * b_blk, :]
    for k in range(CONF_CONV_W - 1):
        glu_s[k * b_blk:(k + 1) * b_blk, :] = glu_s[(t_tile + k) * b_blk:(t_tile + k + 1) * b_blk, :]

    @pl.when(i == pl.num_programs(1) - 1)
    def _():
        h_out_ref[...] = h_s[...]
        c4_out_ref[...] = ux_s[0:p4, :]
        c31_out_ref[...] = glu_s[0:p31, :]


def _mixer_call(x_tm, h0, c4, c31, wts, *, t_tile, b_blk):
    n_b, total_rows, _ = x_tm.shape
    rows = t_tile * b_blk
    n_t = total_rows // rows
    assert n_t * rows == total_rows and b_blk % SUBLANES == 0
    assert t_tile >= RNN_CONV_W - 1
    p4 = (RNN_CONV_W - 1) * b_blk
    p31 = (CONF_CONV_W - 1) * b_blk

    def tile_spec(width):
        return pl.BlockSpec((None, rows, width), lambda j, i: (j, i, 0))

    def state_spec(n_rows):
        return pl.BlockSpec((None, n_rows, D), lambda j, i: (j, 0, 0))

    def const_spec(arr):
        nd = arr.ndim
        return pl.BlockSpec(arr.shape, lambda j, i: (0,) * nd, pipeline_mode=pl.Buffered(1))

    plane = pltpu.VMEM((rows, D), F32)
    return pl.pallas_call(
        functools.partial(_mixer_kernel, t_tile=t_tile, b_blk=b_blk),
        grid=(n_b, n_t),
        in_specs=[tile_spec(D), state_spec(b_blk), state_spec(p4), state_spec(p31)]
                 + [const_spec(w) for w in wts],
        out_specs=[tile_spec(D), tile_spec(D), tile_spec(LANES), state_spec(b_blk), state_spec(p4), state_spec(p31)],
        out_shape=[jax.ShapeDtypeStruct((n_b, total_rows, D), F32),
                   jax.ShapeDtypeStruct((n_b, total_rows, D), BF16),
                   jax.ShapeDtypeStruct((n_b, total_rows, LANES), F32),
                   jax.ShapeDtypeStruct((n_b, b_blk, D), F32),
                   jax.ShapeDtypeStruct((n_b, p4, D), F32),
                   jax.ShapeDtypeStruct((n_b, p31, D), F32)],
        scratch_shapes=[plane, pltpu.VMEM((p4 + rows, D), F32), plane, plane, plane, plane, plane,
                        pltpu.VMEM((p31 + rows, D), F32), plane, plane, plane, plane, plane,
                        pltpu.VMEM((b_blk, D), F32)],
        compiler_params=pltpu.CompilerParams(dimension_semantics=("parallel", "arbitrary"),
                                             vmem_limit_bytes=VMEM_LIMIT),
        name="mixer",
    )(x_tm, h0, c4, c31, *wts)


def _moe_kernel(xn_ref, logit_ref, x1_ref, wg_ref, wu_ref, wd_ref, fg_ref, y_ref, acc_s, comb_s):
    e = pl.program_id(1)
    neg = float("-inf")

    @pl.when(e == 0)
    def _():
        lg = logit_ref[...]
        lane = lax.broadcasted_iota(jnp.int32, lg.shape, 1)
        gmask = lane < N_GROUPS
        gl = jnp.where(gmask, lg, neg)
        gmax = jnp.max(gl, axis=-1, keepdims=True)
        gidx = jnp.min(jnp.where(gl == gmax, lane, LANES), axis=-1, keepdims=True)
        gsum = jnp.sum(jnp.where(gmask, jnp.exp(gl - gmax), 0.0), axis=-1, keepdims=True)
        g_gate = 1.0 / gsum
        first = N_GROUPS + gidx * EXPERTS_PER_GROUP
        emask = (lane >= first) & (lane < first + EXPERTS_PER_GROUP)
        el = jnp.where(emask, lg, neg)
        v1 = jnp.max(el, axis=-1, keepdims=True)
        i1 = jnp.min(jnp.where(el == v1, lane, LANES), axis=-1, keepdims=True)
        el2 = jnp.where(lane == i1, neg, el)
        v2 = jnp.max(el2, axis=-1, keepdims=True)
        i2 = jnp.min(jnp.where(el2 == v2, lane, LANES), axis=-1, keepdims=True)
        e2 = jnp.exp(v2 - v1)
        w1 = 1.0 / (1.0 + e2)
        w2 = e2 / (1.0 + e2)
        comb_s[...] = jnp.where(lane == i1, w1 * g_gate, 0.0) + jnp.where(lane == i2, w2 * g_gate, 0.0)
        acc_s[...] = jnp.zeros_like(acc_s)

    comb = comb_s[...]
    lane = lax.broadcasted_iota(jnp.int32, comb.shape, 1)
    scale = jnp.sum(jnp.where(lane == N_GROUPS + e, comb, 0.0), axis=-1, keepdims=True)
    xb = xn_ref[...]
    he = jax.nn.silu(_dot(xb, wg_ref[...])) * _dot(xb, wu_ref[...])
    acc_s[...] += scale * _dot(he.astype(BF16), wd_ref[...])

    @pl.when(e == pl.num_programs(1) - 1)
    def _():
        y = x1_ref[...] + acc_s[...]
        ms = jnp.mean(y * y, axis=-1, keepdims=True)
        y_ref[...] = y * lax.rsqrt(ms + EPS) * fg_ref[...]


def _moe_call(xn2, logits, x1, wg, wu, wd, fg, *, row_tile):
    n = xn2.shape[0]
    assert n % row_tile == 0
    row = lambda r, e: (r, 0)
    return pl.pallas_call(
        _moe_kernel,
        grid=(n // row_tile, N_EXPERTS),
        in_specs=[pl.BlockSpec((row_tile, D), row), pl.BlockSpec((row_tile, LANES), row),
                  pl.BlockSpec((row_tile, D), row),
                  pl.BlockSpec((None, D, D_EXPERT), lambda r, e: (e, 0, 0)),
                  pl.BlockSpec((None, D, D_EXPERT), lambda r, e: (e, 0, 0)),
                  pl.BlockSpec((None, D_EXPERT, D), lambda r, e: (e, 0, 0)),
                  pl.BlockSpec((1, D), lambda r, e: (0, 0))],
        out_specs=pl.BlockSpec((row_tile, D), row),
        out_shape=jax.ShapeDtypeStruct((n, D), F32),
        scratch_shapes=[pltpu.VMEM((row_tile, D), F32), pltpu.VMEM((row_tile, LANES), F32)],
        compiler_params=pltpu.CompilerParams(dimension_semantics=("parallel", "arbitrary"),
                                             vmem_limit_bytes=VMEM_LIMIT),
        name="moe_dense",
    )(xn2, logits, x1, wg, wu, wd, fg)


def _block_diag_gates(w_a, w_i):
    def bd(w):
        w = w.reshape(N_GATE_BLOCKS, HEADS_PER_BLOCK, HEAD_DIM, HEAD_DIM)
        eye = jnp.eye(HEADS_PER_BLOCK, dtype=w.dtype)
        full = jnp.einsum("bhij,hg->bhigj", w, eye)
        return full.reshape(N_GATE_BLOCKS, MXU_DIM, MXU_DIM)
    return jnp.concatenate([bd(w_a), bd(w_i)], axis=-1).astype(BF16)


def _to_time_major(a, n_b):
    b, t, d = a.shape
    return a.reshape(n_b, b // n_b, t, d).transpose(0, 2, 1, 3).reshape(n_b, t * (b // n_b), d)


def _from_time_major(a, t):
    n_b, rows, d = a.shape
    bb = rows // t
    return a.reshape(n_b, t, bb, d).transpose(0, 2, 1, 3).reshape(n_b * bb, t, d)


def kernel(x_prompt, x_sample, state_rglru_h, state_rglru_conv, state_conformer_conv, meta_tokens, norm1_g, w_in, rnn_conv_w, rnn_conv_b, w_rg_a, b_rg_a, w_rg_i, b_rg_i, rg_lambda, w_rnn_proj, conf_conv_w, conf_conv_b, conf_ln_g, conf_ln_b, w_conv_proj, w_out, norm2_g, w_group, b_group, w_erouter, b_erouter, w_exp_gate, w_exp_up, w_exp_down, final_norm_g):
    assert w_in.shape[0] == 1, "single-layer trunk"
    bsz, seq, _ = x_prompt.shape
    dec_b, dec_t, _ = x_sample.shape

    vec_rows = [norm1_g[0], rnn_conv_b[0], b_rg_a[0], b_rg_i[0], rg_lambda[0], conf_conv_b[0], conf_ln_g[0],
                conf_ln_b[0], norm2_g[0]]
    vec = jnp.zeros((N_VEC_ROWS, D), F32).at[:len(vec_rows)].set(jnp.stack(vec_rows).astype(F32))
    cw4 = jnp.zeros((SUBLANES, D), F32).at[:RNN_CONV_W].set(rnn_conv_w[0])
    cw31 = jnp.zeros((4 * SUBLANES, D), F32).at[:CONF_CONV_W].set(conf_conv_w[0])
    w_router = jnp.zeros((D, LANES), F32)
    w_router = w_router.at[:, :N_GROUPS].set(w_group[0]).at[:, N_GROUPS:N_GROUPS + N_EXPERTS].set(w_erouter[0])
    wr_hi = w_router.astype(BF16)
    wr_lo = (w_router - wr_hi.astype(F32)).astype(BF16)
    b_router = jnp.zeros((1, LANES), F32)
    b_router = b_router.at[0, :N_GROUPS].set(b_group[0]).at[0, N_GROUPS:N_GROUPS + N_EXPERTS].set(
        b_erouter[0].reshape(-1))
    wts = (vec, cw4, cw31, w_in[0].astype(BF16), _block_diag_gates(w_rg_a[0], w_rg_i[0]),
           w_rnn_proj[0].astype(BF16), w_conv_proj[0].astype(BF16), w_out[0].astype(BF16), wr_hi, wr_lo, b_router)
    wg = w_exp_gate[0].astype(BF16)
    wu = w_exp_up[0].astype(BF16)
    wd = w_exp_down[0].astype(BF16)
    fg = final_norm_g.reshape(1, D).astype(F32)

    meta_tm = jnp.broadcast_to(meta_tokens[:, None, :], (N_META, bsz, D)).reshape(1, N_META * bsz, D)
    _, _, _, h_m, c4_m, c31_m = _mixer_call(
        meta_tm, jnp.zeros((1, bsz, D), F32), jnp.zeros((1, (RNN_CONV_W - 1) * bsz, D), F32),
        jnp.zeros((1, (CONF_CONV_W - 1) * bsz, D), F32), wts, t_tile=N_META, b_blk=bsz)

    x1_p, xn2_p, lg_p, h_p, c4_p, c31_p = _mixer_call(
        _to_time_major(x_prompt, 1), h_m, c4_m, c31_m, wts, t_tile=32, b_blk=bsz)
    y_p = _moe_call(xn2_p[0], lg_p[0], x1_p[0], wg, wu, wd, fg, row_tile=512)
    y_prompt = _from_time_major(y_p[None], seq)

    n_b = 4
    bb = dec_b // n_b
    x1_s, xn2_s, lg_s, h_s, c4_s, c31_s = _mixer_call(
        _to_time_major(x_sample, n_b), state_rglru_h[0].reshape(n_b, bb, D),
        _to_time_major(state_rglru_conv[0], n_b), _to_time_major(state_conformer_conv[0], n_b), wts,
        t_tile=dec_t, b_blk=bb)
    n_s = dec_b * dec_t
    y_s = _moe_call(xn2_s.reshape(n_s, D), lg_s.reshape(n_s, LANES), x1_s.reshape(n_s, D), wg, wu, wd, fg,
                    row_tile=512)
    y_sample = _from_time_major(y_s.reshape(n_b, n_s // n_b, D), dec_t)

    return (y_prompt, y_sample,
            h_p.reshape(1, bsz, D), _from_time_major(c4_p, RNN_CONV_W - 1)[None],
            _from_time_major(c31_p, CONF_CONV_W - 1)[None],
            h_s.reshape(1, dec_b, D), _from_time_major(c4_s, RNN_CONV_W - 1)[None],
            _from_time_major(c31_s, CONF_CONV_W - 1)[None])
```

```python
import functools

import jax
import jax.numpy as jnp
from jax import lax
from jax.experimental import pallas as pl
from jax.experimental.pallas import tpu as pltpu

D = 1024
N_META = 16
N_HEADS = 16
HEAD_DIM = D // N_HEADS
RNN_CONV_W = 4
CONF_CONV_W = 31
LRU_C = 8.0
N_GROUPS = 4
EXPERTS_PER_GROUP = 8
N_EXPERTS = N_GROUPS * EXPERTS_PER_GROUP
D_EXPERT = D // 4
EPS = 1e-6

SUBLANES = 8
PACKED_ROWS = 16
LANES = 128
N_COLS = D // LANES
MXU_DIM = 256
HEADS_PER_BLOCK = MXU_DIM // HEAD_DIM
N_GATE_BLOCKS = D // MXU_DIM
VMEM_LIMIT = 56 * 1024 * 1024

F32 = jnp.float32
BF16 = jnp.bfloat16

(V_NORM1, V_CONV4_B, V_BA, V_BI, V_LAMBDA, V_CONV31_B, V_LN_G, V_LN_B, V_NORM2) = range(9)
N_VEC_ROWS = 16


def _dot(a, b):
    return jnp.dot(a, b, preferred_element_type=F32)


def _sigmoid(x):
    return 0.5 * jnp.tanh(0.5 * x) + 0.5


def _row_loop(n_rows, chunk, body, unroll):
    n = n_rows // chunk
    assert n * chunk == n_rows

    def wrapped(s, carry):
        body(pl.multiple_of(s * chunk, chunk))
        return carry
    lax.fori_loop(0, n, wrapped, 0, unroll=min(unroll, n))


def _mixer_kernel(x_ref, h0_ref, c4_ref, c31_ref, vec_ref, cw4_ref, cw31_ref,
                  w_in_ref, w_gate_ref, w_rnn_ref, w_conv_ref, w_out_ref, wr_hi_ref, wr_lo_ref, br_ref,
                  x1_ref, xn2_ref, logit_ref, h_out_ref, c4_out_ref, c31_out_ref,
                  xn_b, ux_s, uy_s, cv_s, cg_s, grnn_s, gconv_s, glu_s, xc_s, xc_b, ga_s, gi_s, hy_s, cn_b, h_s,
                  *, t_tile, b_blk):
    i = pl.program_id(1)
    rows = t_tile * b_blk
    p4 = (RNN_CONV_W - 1) * b_blk
    p31 = (CONF_CONV_W - 1) * b_blk

    @pl.when(i == 0)
    def _():
        h_s[...] = h0_ref[...]
        ux_s[0:p4, :] = c4_ref[...]
        glu_s[0:p31, :] = c31_ref[...]

    def pk(r0):
        return pl.ds(r0, PACKED_ROWS)

    def vec(row):
        return vec_ref[row:row + 1, :]

    def norm1(r0):
        x = x_ref[pk(r0), :]
        ms = jnp.mean(x * x, axis=-1, keepdims=True)
        xn_b[pk(r0), :] = (x * lax.rsqrt(ms + EPS) * vec(V_NORM1)).astype(BF16)
    _row_loop(rows, PACKED_ROWS, norm1, unroll=4)

    ux_s[p4:p4 + rows, :] = _dot(xn_b[...], w_in_ref[:, 0 * D:1 * D])
    uy_s[...] = _dot(xn_b[...], w_in_ref[:, 1 * D:2 * D])
    cv_s[...] = _dot(xn_b[...], w_in_ref[:, 2 * D:3 * D])
    cg_s[...] = _dot(xn_b[...], w_in_ref[:, 3 * D:4 * D])
    grnn_s[...] = _dot(xn_b[...], w_in_ref[:, 4 * D:5 * D])
    gconv_s[...] = _dot(xn_b[...], w_in_ref[:, 5 * D:6 * D])

    def conv4_glu(r0):
        acc = ux_s[pk(r0), :] * cw4_ref[0:1, :]
        for k in range(1, RNN_CONV_W):
            acc = acc + ux_s[pk(r0 + k * b_blk), :] * cw4_ref[k:k + 1, :]
        xc = acc + vec(V_CONV4_B)
        xc_s[pk(r0), :] = xc
        xc_b[pk(r0), :] = xc.astype(BF16)
        glu_s[pk(p31 + r0), :] = cv_s[pk(r0), :] * _sigmoid(cg_s[pk(r0), :])
    _row_loop(rows, PACKED_ROWS, conv4_glu, unroll=2)

    for blk in range(N_GATE_BLOCKS):
        cols = slice(blk * MXU_DIM, (blk + 1) * MXU_DIM)
        g = _dot(xc_b[:, cols], w_gate_ref[blk])
        ga_s[:, cols] = g[:, :MXU_DIM]
        gi_s[:, cols] = g[:, MXU_DIM:]

    c_lam = -LRU_C * jax.nn.softplus(-vec(V_LAMBDA))

    def lru_coeffs(r0):
        r = _sigmoid(ga_s[pk(r0), :] + vec(V_BA))
        ig = _sigmoid(gi_s[pk(r0), :] + vec(V_BI))
        log_a = c_lam * r
        a = jnp.exp(log_a)
        mult = jnp.sqrt(-jnp.tanh(log_a) * (a * a + 1.0))
        ga_s[pk(r0), :] = a
        gi_s[pk(r0), :] = mult * (ig * xc_s[pk(r0), :])
        uy_s[pk(r0), :] = jax.nn.gelu(uy_s[pk(r0), :])
    _row_loop(rows, PACKED_ROWS, lru_coeffs, unroll=2)

    def scan_batch_slab(jb, carry):
        b0 = pl.multiple_of(jb * SUBLANES, SUBLANES)

        def step(t, h):
            sl = pl.ds(pl.multiple_of(t * b_blk + b0, SUBLANES), SUBLANES)
            h = ga_s[sl, :] * h + gi_s[sl, :]
            hy_s[sl, :] = h * uy_s[sl, :]
            return h

        h_s[pl.ds(b0, SUBLANES), :] = lax.fori_loop(0, t_tile, step, h_s[pl.ds(b0, SUBLANES), :],
                                                    unroll=min(8, t_tile))
        return carry
    lax.fori_loop(0, b_blk // SUBLANES, scan_batch_slab, 0)

    strip_t = min(SUBLANES, t_tile)
    n_strip_t = t_tile // strip_t
    assert n_strip_t * strip_t == t_tile

    def conv31_strip(q, carry):
        jb = q // n_strip_t
        tc = q - jb * n_strip_t
        r0 = pl.multiple_of(tc * (strip_t * b_blk) + jb * SUBLANES, SUBLANES)
        for col in range(N_COLS):
            cs = slice(col * LANES, (col + 1) * LANES)
            xs = [glu_s[pl.ds(r0 + j * b_blk, SUBLANES), cs] for j in range(strip_t + CONF_CONV_W - 1)]
            accs = [None] * strip_t
            for k in range(CONF_CONV_W):
                w = cw31_ref[k:k + 1, cs]
                for o in range(strip_t):
                    term = xs[o + k] * w
                    accs[o] = term if k == 0 else accs[o] + term
            bias = vec_ref[V_CONV31_B:V_CONV31_B + 1, cs]
            for o in range(strip_t):
                cv_s[pl.ds(r0 + o * b_blk, SUBLANES), cs] = accs[o] + bias
        return carry
    lax.fori_loop(0, (b_blk // SUBLANES) * n_strip_t, conv31_strip, 0)

    def ln_silu(r0):
        cc = cv_s[pk(r0), :]
        mu = jnp.mean(cc, axis=-1, keepdims=True)
        cen = cc - mu
        var = jnp.mean(cen * cen, axis=-1, keepdims=True)
        y = cen * lax.rsqrt(var + EPS) * vec(V_LN_G) + vec(V_LN_B)
        cn_b[pk(r0), :] = (y * _sigmoid(y)).astype(BF16)
    _row_loop(rows, PACKED_ROWS, ln_silu, unroll=4)

    ga_s[...] = _dot(hy_s[...].astype(BF16), w_rnn_ref[...])
    gi_s[...] = _dot(cn_b[...], w_conv_ref[...])

    def merge(r0):
        xn_b[pk(r0), :] = (_sigmoid(grnn_s[pk(r0), :]) * ga_s[pk(r0), :]
                           + _sigmoid(gconv_s[pk(r0), :]) * gi_s[pk(r0), :]).astype(BF16)
    _row_loop(rows, PACKED_ROWS, merge, unroll=2)
    cg_s[...] = _dot(xn_b[...], w_out_ref[...])

    def residual_norm2(r0):
        x1 = x_ref[pk(r0), :] + cg_s[pk(r0), :]
        x1_ref[pk(r0), :] = x1
        ms = jnp.mean(x1 * x1, axis=-1, keepdims=True)
        xn2 = x1 * lax.rsqrt(ms + EPS) * vec(V_NORM2)
        hi = xn2.astype(BF16)
        xn2_ref[pk(r0), :] = hi
        xc_b[pk(r0), :] = (xn2 - hi.astype(F32)).astype(BF16)
    _row_loop(rows, PACKED_ROWS, residual_norm2, unroll=4)

    logit_ref[...] = (_dot(xn2_ref[...], wr_hi_ref[...]) + _dot(xc_b[...], wr_hi_ref[...])
                      + _dot(xn2_ref[...], wr_lo_ref[...]) + br_ref[...])

    for k in range(RNN_CONV_W - 1):
        ux_s[k * b_blk:(k + 1) * b_blk, :] = ux_s[(t_tile + k) * b_blk:(t_tile + k + 1) * b_blk, :]
    for k in range(CONF_CONV_W - 1):
        glu_s[k * b_blk:(k + 1) * b_blk, :] = glu_s[(t_tile + k) * b_blk:(t_tile + k + 1) * b_blk, :]

    @pl.when(i == pl.num_programs(1) - 1)
    def _():
        h_out_ref[...] = h_s[...]
        c4_out_ref[...] = ux_s[0:p4, :]
        c31_out_ref[...] = glu_s[0:p31, :]


def _mixer_call(x_tm, h0, c4, c31, wts, *, t_tile, b_blk):
    n_b, total_rows, _ = x_tm.shape
    rows = t_tile * b_blk
    n_t = total_rows // rows
    assert n_t * rows == total_rows and b_blk % SUBLANES == 0 and rows % PACKED_ROWS == 0
    assert t_tile >= RNN_CONV_W - 1
    p4 = (RNN_CONV_W - 1) * b_blk
    p31 = (CONF_CONV_W - 1) * b_blk

    def tile_spec(width):
        return pl.BlockSpec((None, rows, width), lambda j, i: (j, i, 0))

    def state_spec(n_rows):
        return pl.BlockSpec((None, n_rows, D), lambda j, i: (j, 0, 0))

    def const_spec(arr):
        nd = arr.ndim
        return pl.BlockSpec(arr.shape, lambda j, i: (0,) * nd, pipeline_mode=pl.Buffered(1))

    plane = pltpu.VMEM((rows, D), F32)
    plane_b = pltpu.VMEM((rows, D), BF16)
    return pl.pallas_call(
        functools.partial(_mixer_kernel, t_tile=t_tile, b_blk=b_blk),
        grid=(n_b, n_t),
        in_specs=[tile_spec(D), state_spec(b_blk), state_spec(p4), state_spec(p31)]
                 + [const_spec(w) for w in wts],
        out_specs=[tile_spec(D), tile_spec(D), tile_spec(LANES), state_spec(b_blk), state_spec(p4), state_spec(p31)],
        out_shape=[jax.ShapeDtypeStruct((n_b, total_rows, D), F32),
                   jax.ShapeDtypeStruct((n_b, total_rows, D), BF16),
                   jax.ShapeDtypeStruct((n_b, total_rows, LANES), F32),
                   jax.ShapeDtypeStruct((n_b, b_blk, D), F32),
                   jax.ShapeDtypeStruct((n_b, p4, D), F32),
                   jax.ShapeDtypeStruct((n_b, p31, D), F32)],
        scratch_shapes=[plane_b, pltpu.VMEM((p4 + rows, D), F32), plane, plane, plane, plane, plane,
                        pltpu.VMEM((p31 + rows, D), F32), plane, plane_b, plane, plane, plane, plane_b,
                        pltpu.VMEM((b_blk, D), F32)],
        compiler_params=pltpu.CompilerParams(dimension_semantics=("parallel", "arbitrary"),
                                             vmem_limit_bytes=VMEM_LIMIT),
        name="mixer",
    )(x_tm, h0, c4, c31, *wts)


def _moe_kernel(xn_ref, logit_ref, x1_ref, wg_ref, wu_ref, wd_ref, fg_ref, y_ref, acc_s, comb_s):
    e = pl.program_id(1)
    neg = float("-inf")

    @pl.when(e == 0)
    def _():
        lg = logit_ref[...]
        lane = lax.broadcasted_iota(jnp.int32, lg.shape, 1)
        gmask = lane < N_GROUPS
        gl = jnp.where(gmask, lg, neg)
        gmax = jnp.max(gl, axis=-1, keepdims=True)
        gidx = jnp.min(jnp.where(gl == gmax, lane, LANES), axis=-1, keepdims=True)
        gsum = jnp.sum(jnp.where(gmask, jnp.exp(gl - gmax), 0.0), axis=-1, keepdims=True)
        g_gate = 1.0 / gsum
        first = N_GROUPS + gidx * EXPERTS_PER_GROUP
        emask = (lane >= first) & (lane < first + EXPERTS_PER_GROUP)
        el = jnp.where(emask, lg, neg)
        v1 = jnp.max(el, axis=-1, keepdims=True)
        i1 = jnp.min(jnp.where(el == v1, lane, LANES), axis=-1, keepdims=True)
        el2 = jnp.where(lane == i1, neg, el)
        v2 = jnp.max(el2, axis=-1, keepdims=True)
        i2 = jnp.min(jnp.where(el2 == v2, lane, LANES), axis=-1, keepdims=True)
        e2 = jnp.exp(v2 - v1)
        w1 = 1.0 / (1.0 + e2)
        w2 = e2 / (1.0 + e2)
        comb_s[...] = jnp.where(lane == i1, w1 * g_gate, 0.0) + jnp.where(lane == i2, w2 * g_gate, 0.0)
        acc_s[...] = jnp.zeros_like(acc_s)

    comb = comb_s[...]
    lane = lax.broadcasted_iota(jnp.int32, comb.shape, 1)
    scale = jnp.sum(jnp.where(lane == N_GROUPS + e, comb, 0.0), axis=-1, keepdims=True)
    xb = xn_ref[...]
    he = jax.nn.silu(_dot(xb, wg_ref[...])) * _dot(xb, wu_ref[...])
    acc_s[...] += scale * _dot(he.astype(BF16), wd_ref[...])

    @pl.when(e == pl.num_programs(1) - 1)
    def _():
        y = x1_ref[...] + acc_s[...]
        ms = jnp.mean(y * y, axis=-1, keepdims=True)
        y_ref[...] = y * lax.rsqrt(ms + EPS) * fg_ref[...]


def _moe_call(xn2, logits, x1, wg, wu, wd, fg, *, row_tile):
    n = xn2.shape[0]
    assert n % row_tile == 0
    row = lambda r, e: (r, 0)
    return pl.pallas_call(
        _moe_kernel,
        grid=(n // row_tile, N_EXPERTS),
        in_specs=[pl.BlockSpec((row_tile, D), row), pl.BlockSpec((row_tile, LANES), row),
                  pl.BlockSpec((row_tile, D), row),
                  pl.BlockSpec((None, D, D_EXPERT), lambda r, e: (e, 0, 0)),
                  pl.BlockSpec((None, D, D_EXPERT), lambda r, e: (e, 0, 0)),
                  pl.BlockSpec((None, D_EXPERT, D), lambda r, e: (e, 0, 0)),
                  pl.BlockSpec((1, D), lambda r, e: (0, 0))],
        out_specs=pl.BlockSpec((row_tile, D), row),
        out_shape=jax.ShapeDtypeStruct((n, D), F32),
        scratch_shapes=[pltpu.VMEM((row_tile, D), F32), pltpu.VMEM((row_tile, LANES), F32)],
        compiler_params=pltpu.CompilerParams(dimension_semantics=("parallel", "arbitrary"),
                                             vmem_limit_bytes=VMEM_LIMIT),
        name="moe_dense",
    )(xn2, logits, x1, wg, wu, wd, fg)


def _block_diag_gates(w_a, w_i):
    def bd(w):
        w = w.reshape(N_GATE_BLOCKS, HEADS_PER_BLOCK, HEAD_DIM, HEAD_DIM)
        eye = jnp.eye(HEADS_PER_BLOCK, dtype=w.dtype)
        full = jnp.einsum("bhij,hg->bhigj", w, eye)
        return full.reshape(N_GATE_BLOCKS, MXU_DIM, MXU_DIM)
    return jnp.concatenate([bd(w_a), bd(w_i)], axis=-1).astype(BF16)


def _to_time_major(a, n_b):
    b, t, d = a.shape
    return a.reshape(n_b, b // n_b, t, d).transpose(0, 2, 1, 3).reshape(n_b, t * (b // n_b), d)


def _from_time_major(a, t):
    n_b, rows, d = a.shape
    bb = rows // t
    return a.reshape(n_b, t, bb, d).transpose(0, 2, 1, 3).reshape(n_b * bb, t, d)


def kernel(x_prompt, x_sample, state_rglru_h, state_rglru_conv, state_conformer_conv, meta_tokens, norm1_g, w_in, rnn_conv_w, rnn_conv_b, w_rg_a, b_rg_a, w_rg_i, b_rg_i, rg_lambda, w_rnn_proj, conf_conv_w, conf_conv_b, conf_ln_g, conf_ln_b, w_conv_proj, w_out, norm2_g, w_group, b_group, w_erouter, b_erouter, w_exp_gate, w_exp_up, w_exp_down, final_norm_g):
    assert w_in.shape[0] == 1, "single-layer trunk"
    bsz, seq, _ = x_prompt.shape
    dec_b, dec_t, _ = x_sample.shape

    vec_rows = [norm1_g[0], rnn_conv_b[0], b_rg_a[0], b_rg_i[0], rg_lambda[0], conf_conv_b[0], conf_ln_g[0],
                conf_ln_b[0], norm2_g[0]]
    vec = jnp.zeros((N_VEC_ROWS, D), F32).at[:len(vec_rows)].set(jnp.stack(vec_rows).astype(F32))
    cw4 = jnp.zeros((SUBLANES, D), F32).at[:RNN_CONV_W].set(rnn_conv_w[0])
    cw31 = jnp.zeros((4 * SUBLANES, D), F32).at[:CONF_CONV_W].set(conf_conv_w[0])
    w_router = jnp.zeros((D, LANES), F32)
    w_router = w_router.at[:, :N_GROUPS].set(w_group[0]).at[:, N_GROUPS:N_GROUPS + N_EXPERTS].set(w_erouter[0])
    wr_hi = w_router.astype(BF16)
    wr_lo = (w_router - wr_hi.astype(F32)).astype(BF16)
    b_router = jnp.zeros((1, LANES), F32)
    b_router = b_router.at[0, :N_GROUPS].set(b_group[0]).at[0, N_GROUPS:N_GROUPS + N_EXPERTS].set(
        b_erouter[0].reshape(-1))
    wts = (vec, cw4, cw31, w_in[0].astype(BF16), _block_diag_gates(w_rg_a[0], w_rg_i[0]),
           w_rnn_proj[0].astype(BF16), w_conv_proj[0].astype(BF16), w_out[0].astype(BF16), wr_hi, wr_lo, b_router)
    wg = w_exp_gate[0].astype(BF16)
    wu = w_exp_up[0].astype(BF16)
    wd = w_exp_down[0].astype(BF16)
    fg = final_norm_g.reshape(1, D).astype(F32)

    meta_tm = jnp.broadcast_to(meta_tokens[:, None, :], (N_META, bsz, D)).reshape(1, N_META * bsz, D)
    _, _, _, h_m, c4_m, c31_m = _mixer_call(
        meta_tm, jnp.zeros((1, bsz, D), F32), jnp.zeros((1, (RNN_CONV_W - 1) * bsz, D), F32),
        jnp.zeros((1, (CONF_CONV_W - 1) * bsz, D), F32), wts, t_tile=N_META, b_blk=bsz)

    x1_p, xn2_p, lg_p, h_p, c4_p, c31_p = _mixer_call(
        _to_time_major(x_prompt, 1), h_m, c4_m, c31_m, wts, t_tile=32, b_blk=bsz)
    y_p = _moe_call(xn2_p[0], lg_p[0], x1_p[0], wg, wu, wd, fg, row_tile=512)
    y_prompt = _from_time_major(y_p[None], seq)

    n_b = 4
    bb = dec_b // n_b
    x1_s, xn2_s, lg_s, h_s, c4_s, c31_s = _mixer_call(
        _to_time_major(x_sample, n_b), state_rglru_h[0].reshape(n_b, bb, D),
        _to_time_major(state_rglru_conv[0], n_b), _to_time_major(state_conformer_conv[0], n_b), wts,
        t_tile=dec_t, b_blk=bb)
    n_s = dec_b * dec_t
    y_s = _moe_call(xn2_s.reshape(n_s, D), lg_s.reshape(n_s, LANES), x1_s.reshape(n_s, D), wg, wu, wd, fg,
                    row_tile=512)
    y_sample = _from_time_major(y_s.reshape(n_b, n_s // n_b, D), dec_t)

    return (y_prompt, y_sample,
            h_p.reshape(1, bsz, D), _from_time_major(c4_p, RNN_CONV_W - 1)[None],
            _from_time_major(c31_p, CONF_CONV_W - 1)[None],
            h_s.reshape(1, dec_b, D), _from_time_major(c4_s, RNN_CONV_W - 1)[None],
            _from_time_major(c31_s, CONF_CONV_W - 1)[None])
```

```python
import functools

import jax
import jax.numpy as jnp
from jax import lax
from jax.experimental import pallas as pl
from jax.experimental.pallas import tpu as pltpu

D = 1024
N_META = 16
N_HEADS = 16
HEAD_DIM = D // N_HEADS
RNN_CONV_W = 4
CONF_CONV_W = 31
LRU_C = 8.0
N_GROUPS = 4
EXPERTS_PER_GROUP = 8
N_EXPERTS = N_GROUPS * EXPERTS_PER_GROUP
D_EXPERT = D // 4
EPS = 1e-6

SUBLANES = 8
PACKED_ROWS = 16
LANES = 128
N_COLS = D // LANES
MXU_DIM = 256
HEADS_PER_BLOCK = MXU_DIM // HEAD_DIM
N_GATE_BLOCKS = D // MXU_DIM
VMEM_LIMIT = 56 * 1024 * 1024
TOKEN_W = D + LANES
EXPERT_LANE0 = N_GROUPS
MOE_TILE = 256
N_PAD_REGIONS = N_GROUPS + 1

F32 = jnp.float32
BF16 = jnp.bfloat16

(V_NORM1, V_CONV4_B, V_BA, V_BI, V_LAMBDA, V_CONV31_B, V_LN_G, V_LN_B, V_NORM2) = range(9)
N_VEC_ROWS = 16


def _dot(a, b):
    return jnp.dot(a, b, preferred_element_type=F32)


def _sigmoid(x):
    return 0.5 * jnp.tanh(0.5 * x) + 0.5


def _row_loop(n_rows, chunk, body, unroll):
    n = n_rows // chunk
    assert n * chunk == n_rows

    def wrapped(s, carry):
        body(pl.multiple_of(s * chunk, chunk))
        return carry
    lax.fori_loop(0, n, wrapped, 0, unroll=min(unroll, n))


def _route_info(lg):
    neg = float("-inf")
    lane = lax.broadcasted_iota(jnp.int32, lg.shape, 1)
    gmask = lane < N_GROUPS
    gl = jnp.where(gmask, lg, neg)
    gmax = jnp.max(gl, axis=-1, keepdims=True)
    gidx = jnp.min(jnp.where(gl == gmax, lane, LANES), axis=-1, keepdims=True)
    gsum = jnp.sum(jnp.where(gmask, jnp.exp(gl - gmax), 0.0), axis=-1, keepdims=True)
    g_gate = 1.0 / gsum
    first = EXPERT_LANE0 + gidx * EXPERTS_PER_GROUP
    emask = (lane >= first) & (lane < first + EXPERTS_PER_GROUP)
    el = jnp.where(emask, lg, neg)
    v1 = jnp.max(el, axis=-1, keepdims=True)
    i1 = jnp.min(jnp.where(el == v1, lane, LANES), axis=-1, keepdims=True)
    el2 = jnp.where(lane == i1, neg, el)
    v2 = jnp.max(el2, axis=-1, keepdims=True)
    i2 = jnp.min(jnp.where(el2 == v2, lane, LANES), axis=-1, keepdims=True)
    e2 = jnp.exp(v2 - v1)
    w1 = 1.0 / (1.0 + e2)
    w2 = e2 / (1.0 + e2)
    comb = jnp.where(lane == i1, w1 * g_gate, 0.0) + jnp.where(lane == i2, w2 * g_gate, 0.0)
    return jnp.where(lane == 0, gidx.astype(F32), comb)


def _mixer_kernel(x_ref, h0_ref, c4_ref, c31_ref, vec_ref, cw4_ref, cw31_ref,
                  w_in_ref, w_gate_ref, w_rnn_ref, w_conv_ref, w_out_ref, wr_hi_ref, wr_lo_ref, br_ref,
                  x1c_ref, h_out_ref, c4_out_ref, c31_out_ref,
                  xn_b, ux_s, uy_s, cv_s, cg_s, grnn_s, gconv_s, glu_s, xc_s, xc_b, ga_s, gi_s, hy_s, cn_b, h_s,
                  *, t_tile, b_blk):
    i = pl.program_id(1)
    rows = t_tile * b_blk
    p4 = (RNN_CONV_W - 1) * b_blk
    p31 = (CONF_CONV_W - 1) * b_blk

    @pl.when(i == 0)
    def _():
        h_s[...] = h0_ref[...]
        ux_s[0:p4, :] = c4_ref[...]
        glu_s[0:p31, :] = c31_ref[...]

    def pk(r0):
        return pl.ds(r0, PACKED_ROWS)

    def vec(row):
        return vec_ref[row:row + 1, :]

    def norm1(r0):
        x = x_ref[pk(r0), :]
        ms = jnp.mean(x * x, axis=-1, keepdims=True)
        xn_b[pk(r0), :] = (x * lax.rsqrt(ms + EPS) * vec(V_NORM1)).astype(BF16)
    _row_loop(rows, PACKED_ROWS, norm1, unroll=4)

    ux_s[p4:p4 + rows, :] = _dot(xn_b[...], w_in_ref[:, 0 * D:1 * D])
    uy_s[...] = _dot(xn_b[...], w_in_ref[:, 1 * D:2 * D])
    cv_s[...] = _dot(xn_b[...], w_in_ref[:, 2 * D:3 * D])
    cg_s[...] = _dot(xn_b[...], w_in_ref[:, 3 * D:4 * D])
    grnn_s[...] = _dot(xn_b[...], w_in_ref[:, 4 * D:5 * D])
    gconv_s[...] = _dot(xn_b[...], w_in_ref[:, 5 * D:6 * D])

    def conv4_glu(r0):
        acc = ux_s[pk(r0), :] * cw4_ref[0:1, :]
        for k in range(1, RNN_CONV_W):
            acc = acc + ux_s[pk(r0 + k * b_blk), :] * cw4_ref[k:k + 1, :]
        xc = acc + vec(V_CONV4_B)
        xc_s[pk(r0), :] = xc
        xc_b[pk(r0), :] = xc.astype(BF16)
        glu_s[pk(p31 + r0), :] = cv_s[pk(r0), :] * _sigmoid(cg_s[pk(r0), :])
    _row_loop(rows, PACKED_ROWS, conv4_glu, unroll=2)

    for blk in range(N_GATE_BLOCKS):
        cols = slice(blk * MXU_DIM, (blk + 1) * MXU_DIM)
        g = _dot(xc_b[:, cols], w_gate_ref[blk])
        ga_s[:, cols] = g[:, :MXU_DIM]
        gi_s[:, cols] = g[:, MXU_DIM:]

    c_lam = -LRU_C * jax.nn.softplus(-vec(V_LAMBDA))

    def lru_coeffs(r0):
        r = _sigmoid(ga_s[pk(r0), :] + vec(V_BA))
        ig = _sigmoid(gi_s[pk(r0), :] + vec(V_BI))
        log_a = c_lam * r
        a = jnp.exp(log_a)
        mult = jnp.sqrt(-jnp.tanh(log_a) * (a * a + 1.0))
        ga_s[pk(r0), :] = a
        gi_s[pk(r0), :] = mult * (ig * xc_s[pk(r0), :])
        uy_s[pk(r0), :] = jax.nn.gelu(uy_s[pk(r0), :])
    _row_loop(rows, PACKED_ROWS, lru_coeffs, unroll=2)

    def scan_batch_slab(jb, carry):
        b0 = pl.multiple_of(jb * SUBLANES, SUBLANES)

        def step(t, h):
            sl = pl.ds(pl.multiple_of(t * b_blk + b0, SUBLANES), SUBLANES)
            h = ga_s[sl, :] * h + gi_s[sl, :]
            hy_s[sl, :] = h * uy_s[sl, :]
            return h

        h_s[pl.ds(b0, SUBLANES), :] = lax.fori_loop(0, t_tile, step, h_s[pl.ds(b0, SUBLANES), :],
                                                    unroll=min(8, t_tile))
        return carry
    lax.fori_loop(0, b_blk // SUBLANES, scan_batch_slab, 0)

    strip_t = min(SUBLANES, t_tile)
    n_strip_t = t_tile // strip_t
    assert n_strip_t * strip_t == t_tile

    def conv31_strip(q, carry):
        jb = q // n_strip_t
        tc = q - jb * n_strip_t
        r0 = pl.multiple_of(tc * (strip_t * b_blk) + jb * SUBLANES, SUBLANES)
        for col in range(N_COLS):
            cs = slice(col * LANES, (col + 1) * LANES)
            xs = [glu_s[pl.ds(r0 + j * b_blk, SUBLANES), cs] for j in range(strip_t + CONF_CONV_W - 1)]
            accs = [None] * strip_t
            for k in range(CONF_CONV_W):
                w = cw31_ref[k:k + 1, cs]
                for o in range(strip_t):
                    term = xs[o + k] * w
                    accs[o] = term if k == 0 else accs[o] + term
            bias = vec_ref[V_CONV31_B:V_CONV31_B + 1, cs]
            for o in range(strip_t):
                cv_s[pl.ds(r0 + o * b_blk, SUBLANES), cs] = accs[o] + bias
        return carry
    lax.fori_loop(0, (b_blk // SUBLANES) * n_strip_t, conv31_strip, 0)

    def ln_silu(r0):
        cc = cv_s[pk(r0), :]
        mu = jnp.mean(cc, axis=-1, keepdims=True)
        cen = cc - mu
        var = jnp.mean(cen * cen, axis=-1, keepdims=True)
        y = cen * lax.rsqrt(var + EPS) * vec(V_LN_G) + vec(V_LN_B)
        cn_b[pk(r0), :] = (y * _sigmoid(y)).astype(BF16)
    _row_loop(rows, PACKED_ROWS, ln_silu, unroll=4)

    ga_s[...] = _dot(hy_s[...].astype(BF16), w_rnn_ref[...])
    gi_s[...] = _dot(cn_b[...], w_conv_ref[...])

    def merge(r0):
        xn_b[pk(r0), :] = (_sigmoid(grnn_s[pk(r0), :]) * ga_s[pk(r0), :]
                           + _sigmoid(gconv_s[pk(r0), :]) * gi_s[pk(r0), :]).astype(BF16)
    _row_loop(rows, PACKED_ROWS, merge, unroll=2)
    cg_s[...] = _dot(xn_b[...], w_out_ref[...])

    def residual_norm2(r0):
        x1 = x_ref[pk(r0), :] + cg_s[pk(r0), :]
        x1c_ref[pk(r0), 0:D] = x1
        ms = jnp.mean(x1 * x1, axis=-1, keepdims=True)
        xn2 = x1 * lax.rsqrt(ms + EPS) * vec(V_NORM2)
        hi = xn2.astype(BF16)
        cn_b[pk(r0), :] = hi
        xc_b[pk(r0), :] = (xn2 - hi.astype(F32)).astype(BF16)
    _row_loop(rows, PACKED_ROWS, residual_norm2, unroll=4)

    logits = (_dot(cn_b[...], wr_hi_ref[...]) + _dot(xc_b[...], wr_hi_ref[...])
              + _dot(cn_b[...], wr_lo_ref[...]) + br_ref[...])
    x1c_ref[:, D:D + LANES] = _route_info(logits)

    for k in range(RNN_CONV_W - 1):
        ux_s[k * b_blk:(k + 1) * b_blk, :] = ux_s[(t_tile + k) * b_blk:(t_tile + k + 1) * b_blk, :]
    for k in range(CONF_CONV_W - 1):
        glu_s[k * b_blk:(k + 1) * b_blk, :] = glu_s[(t_tile + k) * b_blk:(t_tile + k + 1) * b_blk, :]

    @pl.when(i == pl.num_programs(1) - 1)
    def _():
        h_out_ref[...] = h_s[...]
        c4_out_ref[...] = ux_s[0:p4, :]
        c31_out_ref[...] = glu_s[0:p31, :]


def _mixer_call(x_tm, h0, c4, c31, wts, *, t_tile, b_blk):
    n_b, total_rows, _ = x_tm.shape
    rows = t_tile * b_blk
    n_t = total_rows // rows
    assert n_t * rows == total_rows and b_blk % SUBLANES == 0 and rows % PACKED_ROWS == 0
    assert t_tile >= RNN_CONV_W - 1
    p4 = (RNN_CONV_W - 1) * b_blk
    p31 = (CONF_CONV_W - 1) * b_blk

    def tile_spec(width):
        return pl.BlockSpec((None, rows, width), lambda j, i: (j, i, 0))

    def state_spec(n_rows):
        return pl.BlockSpec((None, n_rows, D), lambda j, i: (j, 0, 0))

    def const_spec(arr):
        nd = arr.ndim
        return pl.BlockSpec(arr.shape, lambda j, i: (0,) * nd, pipeline_mode=pl.Buffered(1))

    plane = pltpu.VMEM((rows, D), F32)
    plane_b = pltpu.VMEM((rows, D), BF16)
    return pl.pallas_call(
        functools.partial(_mixer_kernel, t_tile=t_tile, b_blk=b_blk),
        grid=(n_b, n_t),
        in_specs=[tile_spec(D), state_spec(b_blk), state_spec(p4), state_spec(p31)]
                 + [const_spec(w) for w in wts],
        out_specs=[pl.BlockSpec((rows, TOKEN_W), lambda j, i: (j * n_t + i, 0)),
                   state_spec(b_blk), state_spec(p4), state_spec(p31)],
        out_shape=[jax.ShapeDtypeStruct((n_b * total_rows, TOKEN_W), F32),
                   jax.ShapeDtypeStruct((n_b, b_blk, D), F32),
                   jax.ShapeDtypeStruct((n_b, p4, D), F32),
                   jax.ShapeDtypeStruct((n_b, p31, D), F32)],
        scratch_shapes=[plane_b, pltpu.VMEM((p4 + rows, D), F32), plane, plane, plane, plane, plane,
                        pltpu.VMEM((p31 + rows, D), F32), plane, plane_b, plane, plane, plane, plane_b,
                        pltpu.VMEM((b_blk, D), F32)],
        compiler_params=pltpu.CompilerParams(dimension_semantics=("arbitrary", "arbitrary"),
                                             vmem_limit_bytes=VMEM_LIMIT),
        name="mixer",
    )(x_tm, h0, c4, c31, *wts)


def _row_copies(n, make_copy):
    def issue(r, carry):
        make_copy(r).start()
        return carry
    lax.fori_loop(0, n, issue, 0, unroll=8)


def _dispatch_kernel(slot_ref, pad0_ref, padn_ref, tok_a_ref, tok_b_ref, sorted_hbm, zero_s, sem, pad_sem,
                     *, n_a_tiles):
    i = pl.program_id(0)

    def scatter(tok_ref):
        def row_copy(r):
            return pltpu.make_async_copy(tok_ref.at[pl.ds(r, 1), :],
                                         sorted_hbm.at[pl.ds(slot_ref[i * MOE_TILE + r], 1), :], sem)
        _row_copies(MOE_TILE, row_copy)
        pltpu.make_async_copy(tok_ref, sorted_hbm.at[pl.ds(0, MOE_TILE), :], sem).wait()

    @pl.when(i < n_a_tiles)
    def _():
        scatter(tok_a_ref)

    @pl.when(i >= n_a_tiles)
    def _():
        scatter(tok_b_ref)

    @pl.when(i == pl.num_programs(0) - 1)
    def _():
        zero_s[...] = jnp.zeros_like(zero_s)
        for region in range(N_PAD_REGIONS):
            def pad_copy(k, region=region):
                return pltpu.make_async_copy(zero_s.at[pl.ds(0, 1), :],
                                             sorted_hbm.at[pl.ds(pad0_ref[region] + k, 1), :], pad_sem)

            def issue(k, carry):
                pad_copy(k).start()
                return carry

            def drain(k, carry):
                pad_copy(k).wait()
                return carry
            lax.fori_loop(0, padn_ref[region], issue, 0)
            lax.fori_loop(0, padn_ref[region], drain, 0)


def _dispatch_call(tokens_a, tokens_b, slot, pad0, padn, *, n_sorted):
    n_a, n_b = tokens_a.shape[0], tokens_b.shape[0]
    assert n_a % MOE_TILE == 0 and n_b % MOE_TILE == 0
    n_a_tiles = n_a // MOE_TILE
    return pl.pallas_call(
        functools.partial(_dispatch_kernel, n_a_tiles=n_a_tiles),
        grid_spec=pltpu.PrefetchScalarGridSpec(
            num_scalar_prefetch=3, grid=((n_a + n_b) // MOE_TILE,),
            in_specs=[pl.BlockSpec((MOE_TILE, TOKEN_W), lambda i, *_: (jnp.minimum(i, n_a_tiles - 1), 0)),
                      pl.BlockSpec((MOE_TILE, TOKEN_W), lambda i, *_: (jnp.maximum(i - n_a_tiles, 0), 0))],
            out_specs=pl.BlockSpec(memory_space=pl.ANY),
            scratch_shapes=[pltpu.VMEM((SUBLANES, TOKEN_W), F32), pltpu.SemaphoreType.DMA(()),
                            pltpu.SemaphoreType.DMA(())]),
        out_shape=jax.ShapeDtypeStruct((n_sorted, TOKEN_W), F32),
        compiler_params=pltpu.CompilerParams(dimension_semantics=("arbitrary",)),
        name="moe_dispatch",
    )(slot, pad0, padn, tokens_a, tokens_b)


def _moe_kernel(tile_group_ref, n_tiles_ref, tok_ref, wg_ref, wu_ref, wd_ref, gains_ref, y_ref, xn_b, acc_s):
    i = pl.program_id(0)
    e = pl.program_id(1)
    live = i < n_tiles_ref[0]

    @pl.when(live & (e == 0))
    def _():
        x1 = tok_ref[:, 0:D]
        ms = jnp.mean(x1 * x1, axis=-1, keepdims=True)
        xn_b[...] = (x1 * lax.rsqrt(ms + EPS) * gains_ref[0:1, :]).astype(BF16)
        acc_s[...] = jnp.zeros_like(acc_s)

    @pl.when(live)
    def _():
        info = tok_ref[:, D:D + LANES]
        lane = lax.broadcasted_iota(jnp.int32, info.shape, 1)
        expert_lane = EXPERT_LANE0 + tile_group_ref[i] * EXPERTS_PER_GROUP + e
        scale = jnp.sum(jnp.where(lane == expert_lane, info, 0.0), axis=-1, keepdims=True)
        xb = xn_b[...]
        gate = _dot(xb, wg_ref[...])
        he = gate * _sigmoid(gate) * _dot(xb, wu_ref[...])
        acc_s[...] += scale * _dot(he.astype(BF16), wd_ref[...])

    @pl.when(live & (e == pl.num_programs(1) - 1))
    def _():
        y = tok_ref[:, 0:D] + acc_s[...]
        ms = jnp.mean(y * y, axis=-1, keepdims=True)
        y_ref[...] = y * lax.rsqrt(ms + EPS) * gains_ref[1:2, :]

    @pl.when(jnp.logical_not(live) & (e == 0))
    def _():
        y_ref[...] = jnp.zeros_like(y_ref)


def _moe_call(sorted_tokens, tile_group, n_tiles, wg, wu, wd, gains):
    n_sorted = sorted_tokens.shape[0]
    assert n_sorted % MOE_TILE == 0

    def tile(i, e, tile_group_ref, n_tiles_ref):
        return (jnp.minimum(i, n_tiles_ref[0] - 1), 0)

    def expert(i, e, tile_group_ref, n_tiles_ref):
        live = i < n_tiles_ref[0]
        group = tile_group_ref[jnp.minimum(i, n_tiles_ref[0] - 1)]
        return (group * EXPERTS_PER_GROUP + jnp.where(live, e, EXPERTS_PER_GROUP - 1), 0, 0)

    return pl.pallas_call(
        _moe_kernel,
        grid_spec=pltpu.PrefetchScalarGridSpec(
            num_scalar_prefetch=2, grid=(n_sorted // MOE_TILE, EXPERTS_PER_GROUP),
            in_specs=[pl.BlockSpec((MOE_TILE, TOKEN_W), tile),
                      pl.BlockSpec((None, D, D_EXPERT), expert),
                      pl.BlockSpec((None, D, D_EXPERT), expert),
                      pl.BlockSpec((None, D_EXPERT, D), expert),
                      pl.BlockSpec((SUBLANES, D), lambda i, e, *_: (0, 0))],
            out_specs=pl.BlockSpec((MOE_TILE, D), lambda i, e, *_: (i, 0)),
            scratch_shapes=[pltpu.VMEM((MOE_TILE, D), BF16), pltpu.VMEM((MOE_TILE, D), F32)]),
        out_shape=jax.ShapeDtypeStruct((n_sorted, D), F32),
        compiler_params=pltpu.CompilerParams(dimension_semantics=("arbitrary", "arbitrary"),
                                             vmem_limit_bytes=VMEM_LIMIT),
        name="moe_grouped",
    )(tile_group, n_tiles, sorted_tokens, wg, wu, wd, gains)


def _collect_kernel(src_ref, y_hbm, o_ref, sem):
    i = pl.program_id(0)

    def row_copy(r):
        return pltpu.make_async_copy(y_hbm.at[pl.ds(src_ref[i * MOE_TILE + r], 1), :],
                                     o_ref.at[pl.ds(r, 1), :], sem)
    _row_copies(MOE_TILE, row_copy)
    pltpu.make_async_copy(y_hbm.at[pl.ds(0, MOE_TILE), :], o_ref, sem).wait()


def _collect_call(y_sorted, src):
    n = src.shape[0]
    assert n % MOE_TILE == 0
    return pl.pallas_call(
        _collect_kernel,
        grid_spec=pltpu.PrefetchScalarGridSpec(
            num_scalar_prefetch=1, grid=(n // MOE_TILE,),
            in_specs=[pl.BlockSpec(memory_space=pl.ANY)],
            out_specs=pl.BlockSpec((MOE_TILE, D), lambda i, *_: (i, 0)),
            scratch_shapes=[pltpu.SemaphoreType.DMA(())]),
        out_shape=jax.ShapeDtypeStruct((n, D), F32),
        compiler_params=pltpu.CompilerParams(dimension_semantics=("arbitrary",)),
        name="moe_collect",
    )(src, y_sorted)


def _routing_tables(group, n_tile_cap):
    onehot = (group[:, None] == jnp.arange(N_GROUPS, dtype=jnp.int32)[None, :]).astype(jnp.int32)
    running = jnp.cumsum(onehot, axis=0)
    rank = jnp.sum(running * onehot, axis=1) - 1
    counts = running[-1]
    tiles_per_group = (counts + MOE_TILE - 1) // MOE_TILE
    end_tile = jnp.cumsum(tiles_per_group)
    first_row = (end_tile - tiles_per_group) * MOE_TILE
    slot = jnp.sum(first_row[None, :] * onehot, axis=1) + rank
    tile_ids = jnp.arange(n_tile_cap, dtype=jnp.int32)
    tile_group = jnp.minimum(jnp.sum((tile_ids[:, None] >= end_tile[None, :]).astype(jnp.int32), axis=1),
                             N_GROUPS - 1)
    used_rows = end_tile[-1:] * MOE_TILE
    pad0 = jnp.concatenate([first_row + counts, used_rows])
    padn = jnp.concatenate([tiles_per_group * MOE_TILE - counts, n_tile_cap * MOE_TILE - used_rows])
    return (slot.astype(jnp.int32), tile_group.astype(jnp.int32), end_tile[-1:].astype(jnp.int32),
            pad0.astype(jnp.int32), padn.astype(jnp.int32))


def _block_diag_gates(w_a, w_i):
    def bd(w):
        w = w.reshape(N_GATE_BLOCKS, HEADS_PER_BLOCK, HEAD_DIM, HEAD_DIM)
        eye = jnp.eye(HEADS_PER_BLOCK, dtype=w.dtype)
        full = jnp.einsum("bhij,hg->bhigj", w, eye)
        return full.reshape(N_GATE_BLOCKS, MXU_DIM, MXU_DIM)
    return jnp.concatenate([bd(w_a), bd(w_i)], axis=-1).astype(BF16)


def _to_time_major(a, n_b):
    b, t, d = a.shape
    return a.reshape(n_b, b // n_b, t, d).transpose(0, 2, 1, 3).reshape(n_b, t * (b // n_b), d)


def _from_time_major(a, t):
    n_b, rows, d = a.shape
    bb = rows // t
    return a.reshape(n_b, t, bb, d).transpose(0, 2, 1, 3).reshape(n_b * bb, t, d)


def kernel(x_prompt, x_sample, state_rglru_h, state_rglru_conv, state_conformer_conv, meta_tokens, norm1_g, w_in, rnn_conv_w, rnn_conv_b, w_rg_a, b_rg_a, w_rg_i, b_rg_i, rg_lambda, w_rnn_proj, conf_conv_w, conf_conv_b, conf_ln_g, conf_ln_b, w_conv_proj, w_out, norm2_g, w_group, b_group, w_erouter, b_erouter, w_exp_gate, w_exp_up, w_exp_down, final_norm_g):
    assert w_in.shape[0] == 1, "single-layer trunk"
    bsz, seq, _ = x_prompt.shape
    dec_b, dec_t, _ = x_sample.shape

    vec_rows = [norm1_g[0], rnn_conv_b[0], b_rg_a[0], b_rg_i[0], rg_lambda[0], conf_conv_b[0], conf_ln_g[0],
                conf_ln_b[0], norm2_g[0]]
    vec = jnp.zeros((N_VEC_ROWS, D), F32).at[:len(vec_rows)].set(jnp.stack(vec_rows).astype(F32))
    cw4 = jnp.zeros((SUBLANES, D), F32).at[:RNN_CONV_W].set(rnn_conv_w[0])
    cw31 = jnp.zeros((4 * SUBLANES, D), F32).at[:CONF_CONV_W].set(conf_conv_w[0])
    w_router = jnp.zeros((D, LANES), F32)
    w_router = w_router.at[:, :N_GROUPS].set(w_group[0]).at[:, N_GROUPS:N_GROUPS + N_EXPERTS].set(w_erouter[0])
    wr_hi = w_router.astype(BF16)
    wr_lo = (w_router - wr_hi.astype(F32)).astype(BF16)
    b_router = jnp.zeros((1, LANES), F32)
    b_router = b_router.at[0, :N_GROUPS].set(b_group[0]).at[0, N_GROUPS:N_GROUPS + N_EXPERTS].set(
        b_erouter[0].reshape(-1))
    wts = (vec, cw4, cw31, w_in[0].astype(BF16), _block_diag_gates(w_rg_a[0], w_rg_i[0]),
           w_rnn_proj[0].astype(BF16), w_conv_proj[0].astype(BF16), w_out[0].astype(BF16), wr_hi, wr_lo, b_router)
    wg = w_exp_gate[0].astype(BF16)
    wu = w_exp_up[0].astype(BF16)
    wd = w_exp_down[0].astype(BF16)
    gains = jnp.zeros((SUBLANES, D), F32).at[0].set(norm2_g[0]).at[1].set(final_norm_g)

    n_p = bsz * seq
    n_s = dec_b * dec_t

    meta_tm = jnp.broadcast_to(meta_tokens[:, None, :], (N_META, bsz, D)).reshape(1, N_META * bsz, D)
    _, h_m, c4_m, c31_m = _mixer_call(
        meta_tm, jnp.zeros((1, bsz, D), F32), jnp.zeros((1, (RNN_CONV_W - 1) * bsz, D), F32),
        jnp.zeros((1, (CONF_CONV_W - 1) * bsz, D), F32), wts, t_tile=N_META, b_blk=bsz)

    tok_p, h_p, c4_p, c31_p = _mixer_call(
        _to_time_major(x_prompt, 1), h_m, c4_m, c31_m, wts, t_tile=32, b_blk=bsz)

    n_b = 4
    bb = dec_b // n_b
    tok_s, h_s, c4_s, c31_s = _mixer_call(
        _to_time_major(x_sample, n_b), state_rglru_h[0].reshape(n_b, bb, D),
        _to_time_major(state_rglru_conv[0], n_b), _to_time_major(state_conformer_conv[0], n_b), wts,
        t_tile=dec_t, b_blk=bb)

    n_tile_cap = (n_p + n_s) // MOE_TILE + N_GROUPS
    group = jnp.concatenate([tok_p[:, D], tok_s[:, D]]).astype(jnp.int32)
    slot, tile_group, n_tiles, pad0, padn = _routing_tables(group, n_tile_cap)
    sorted_tokens = _dispatch_call(tok_p, tok_s, slot, pad0, padn, n_sorted=n_tile_cap * MOE_TILE)
    y_sorted = _moe_call(sorted_tokens, tile_group, n_tiles, wg, wu, wd, gains)
    src_p = slot[:n_p].reshape(seq, bsz).T.reshape(-1)
    src_s = slot[n_p:].reshape(n_b, dec_t, bb).transpose(0, 2, 1).reshape(-1)
    y_prompt = _collect_call(y_sorted, src_p).reshape(bsz, seq, D)
    y_sample = _collect_call(y_sorted, src_s).reshape(dec_b, dec_t, D)

    return (y_prompt, y_sample,
            h_p.reshape(1, bsz, D), _from_time_major(c4_p, RNN_CONV_W - 1)[None],
            _from_time_major(c31_p, CONF_CONV_W - 1)[None],
            h_s.reshape(1, dec_b, D), _from_time_major(c4_s, RNN_CONV_W - 1)[None],
            _from_time_major(c31_s, CONF_CONV_W - 1)[None])
```

```python
import functools

import jax
import jax.numpy as jnp
from jax import lax
from jax.experimental import pallas as pl
from jax.experimental.pallas import tpu as pltpu

D = 1024
N_META = 16
N_HEADS = 16
HEAD_DIM = D // N_HEADS
RNN_CONV_W = 4
CONF_CONV_W = 31
LRU_C = 8.0
N_GROUPS = 4
EXPERTS_PER_GROUP = 8
N_EXPERTS = N_GROUPS * EXPERTS_PER_GROUP
D_EXPERT = D // 4
EPS = 1e-6

SUBLANES = 8
PACKED_ROWS = 16
LANES = 128
N_COLS = D // LANES
MXU_DIM = 256
HEADS_PER_BLOCK = MXU_DIM // HEAD_DIM
N_GATE_BLOCKS = D // MXU_DIM
VMEM_LIMIT = 56 * 1024 * 1024
TOKEN_W = D + LANES
EXPERT_LANE0 = N_GROUPS
MOE_TILE = 256
N_PAD_REGIONS = N_GROUPS + 1

F32 = jnp.float32
BF16 = jnp.bfloat16

(V_NORM1, V_CONV4_B, V_BA, V_BI, V_LAMBDA, V_CONV31_B, V_LN_G, V_LN_B, V_NORM2) = range(9)
N_VEC_ROWS = 16


def _dot(a, b):
    return jnp.dot(a, b, preferred_element_type=F32)


def _sigmoid(x):
    return 0.5 * jnp.tanh(0.5 * x) + 0.5


def _row_loop(n_rows, chunk, body, unroll):
    n = n_rows // chunk
    assert n * chunk == n_rows

    def wrapped(s, carry):
        body(pl.multiple_of(s * chunk, chunk))
        return carry
    lax.fori_loop(0, n, wrapped, 0, unroll=min(unroll, n))


def _route_info(lg):
    neg = float("-inf")
    lane = lax.broadcasted_iota(jnp.int32, lg.shape, 1)
    gmask = lane < N_GROUPS
    gl = jnp.where(gmask, lg, neg)
    gmax = jnp.max(gl, axis=-1, keepdims=True)
    gidx = jnp.min(jnp.where(gl == gmax, lane, LANES), axis=-1, keepdims=True)
    gsum = jnp.sum(jnp.where(gmask, jnp.exp(gl - gmax), 0.0), axis=-1, keepdims=True)
    g_gate = 1.0 / gsum
    first = EXPERT_LANE0 + gidx * EXPERTS_PER_GROUP
    emask = (lane >= first) & (lane < first + EXPERTS_PER_GROUP)
    el = jnp.where(emask, lg, neg)
    v1 = jnp.max(el, axis=-1, keepdims=True)
    i1 = jnp.min(jnp.where(el == v1, lane, LANES), axis=-1, keepdims=True)
    el2 = jnp.where(lane == i1, neg, el)
    v2 = jnp.max(el2, axis=-1, keepdims=True)
    i2 = jnp.min(jnp.where(el2 == v2, lane, LANES), axis=-1, keepdims=True)
    e2 = jnp.exp(v2 - v1)
    w1 = 1.0 / (1.0 + e2)
    w2 = e2 / (1.0 + e2)
    comb = jnp.where(lane == i1, w1 * g_gate, 0.0) + jnp.where(lane == i2, w2 * g_gate, 0.0)
    return jnp.where(lane == 0, gidx.astype(F32), comb)


def _mixer_kernel(x_ref, h0_ref, c4_ref, c31_ref, vec_ref, cw4_ref, cw31_ref,
                  w_in_ref, w_gate_ref, w_rnn_ref, w_conv_ref, w_out_ref, wr_hi_ref, wr_lo_ref, br_ref,
                  x1c_ref, h_out_ref, c4_out_ref, c31_out_ref,
                  xn_b, ux_s, uy_s, cv_s, cg_s, grnn_s, gconv_s, glu_s, xc_s, xc_b, ga_s, gi_s, hy_s, cn_b, h_s,
                  *, t_tile, b_blk, batch_major):
    i = pl.program_id(1)
    rows = t_tile * b_blk
    p4 = (RNN_CONV_W - 1) * b_blk
    p31 = (CONF_CONV_W - 1) * b_blk

    @pl.when(i == 0)
    def _():
        h_s[...] = h0_ref[...]
        ux_s[0:p4, :] = c4_ref[...]
        glu_s[0:p31, :] = c31_ref[...]

    def pk(r0):
        return pl.ds(r0, PACKED_ROWS)

    def vec(row):
        return vec_ref[row:row + 1, :]

    def x_rows(r0):
        if not batch_major:
            return x_ref[pk(r0), :]
        t = r0 // b_blk
        return jnp.concatenate([x_ref[:, t + k, :] for k in range(PACKED_ROWS // b_blk)], axis=0)

    def norm1(r0):
        x = x_rows(r0)
        ms = jnp.mean(x * x, axis=-1, keepdims=True)
        xn_b[pk(r0), :] = (x * lax.rsqrt(ms + EPS) * vec(V_NORM1)).astype(BF16)
    _row_loop(rows, PACKED_ROWS, norm1, unroll=4)

    ux_s[p4:p4 + rows, :] = _dot(xn_b[...], w_in_ref[:, 0 * D:1 * D])
    uy_s[...] = _dot(xn_b[...], w_in_ref[:, 1 * D:2 * D])
    cv_s[...] = _dot(xn_b[...], w_in_ref[:, 2 * D:3 * D])
    cg_s[...] = _dot(xn_b[...], w_in_ref[:, 3 * D:4 * D])
    grnn_s[...] = _dot(xn_b[...], w_in_ref[:, 4 * D:5 * D])
    gconv_s[...] = _dot(xn_b[...], w_in_ref[:, 5 * D:6 * D])

    def conv4_glu(r0):
        acc = ux_s[pk(r0), :] * cw4_ref[0:1, :]
        for k in range(1, RNN_CONV_W):
            acc = acc + ux_s[pk(r0 + k * b_blk), :] * cw4_ref[k:k + 1, :]
        xc = acc + vec(V_CONV4_B)
        xc_s[pk(r0), :] = xc
        xc_b[pk(r0), :] = xc.astype(BF16)
        glu_s[pk(p31 + r0), :] = cv_s[pk(r0), :] * _sigmoid(cg_s[pk(r0), :])
    _row_loop(rows, PACKED_ROWS, conv4_glu, unroll=2)

    for blk in range(N_GATE_BLOCKS):
        cols = slice(blk * MXU_DIM, (blk + 1) * MXU_DIM)
        g = _dot(xc_b[:, cols], w_gate_ref[blk])
        ga_s[:, cols] = g[:, :MXU_DIM]
        gi_s[:, cols] = g[:, MXU_DIM:]

    c_lam = -LRU_C * jax.nn.softplus(-vec(V_LAMBDA))

    def lru_coeffs(r0):
        r = _sigmoid(ga_s[pk(r0), :] + vec(V_BA))
        ig = _sigmoid(gi_s[pk(r0), :] + vec(V_BI))
        log_a = c_lam * r
        a = jnp.exp(log_a)
        mult = jnp.sqrt(-jnp.tanh(log_a) * (a * a + 1.0))
        ga_s[pk(r0), :] = a
        gi_s[pk(r0), :] = mult * (ig * xc_s[pk(r0), :])
        uy_s[pk(r0), :] = jax.nn.gelu(uy_s[pk(r0), :])
    _row_loop(rows, PACKED_ROWS, lru_coeffs, unroll=2)

    def scan_batch_slab(jb, carry):
        b0 = pl.multiple_of(jb * SUBLANES, SUBLANES)

        def step(t, h):
            sl = pl.ds(pl.multiple_of(t * b_blk + b0, SUBLANES), SUBLANES)
            h = ga_s[sl, :] * h + gi_s[sl, :]
            hy_s[sl, :] = h * uy_s[sl, :]
            return h

        h_s[pl.ds(b0, SUBLANES), :] = lax.fori_loop(0, t_tile, step, h_s[pl.ds(b0, SUBLANES), :],
                                                    unroll=min(8, t_tile))
        return carry
    lax.fori_loop(0, b_blk // SUBLANES, scan_batch_slab, 0)

    strip_t = min(SUBLANES, t_tile)
    n_strip_t = t_tile // strip_t
    assert n_strip_t * strip_t == t_tile

    def conv31_strip(q, carry):
        jb = q // n_strip_t
        tc = q - jb * n_strip_t
        r0 = pl.multiple_of(tc * (strip_t * b_blk) + jb * SUBLANES, SUBLANES)
        for col in range(N_COLS):
            cs = slice(col * LANES, (col + 1) * LANES)
            xs = [glu_s[pl.ds(r0 + j * b_blk, SUBLANES), cs] for j in range(strip_t + CONF_CONV_W - 1)]
            accs = [None] * strip_t
            for k in range(CONF_CONV_W):
                w = cw31_ref[k:k + 1, cs]
                for o in range(strip_t):
                    term = xs[o + k] * w
                    accs[o] = term if k == 0 else accs[o] + term
            bias = vec_ref[V_CONV31_B:V_CONV31_B + 1, cs]
            for o in range(strip_t):
                cv_s[pl.ds(r0 + o * b_blk, SUBLANES), cs] = accs[o] + bias
        return carry
    lax.fori_loop(0, (b_blk // SUBLANES) * n_strip_t, conv31_strip, 0)

    def ln_silu(r0):
        cc = cv_s[pk(r0), :]
        mu = jnp.mean(cc, axis=-1, keepdims=True)
        cen = cc - mu
        var = jnp.mean(cen * cen, axis=-1, keepdims=True)
        y = cen * lax.rsqrt(var + EPS) * vec(V_LN_G) + vec(V_LN_B)
        cn_b[pk(r0), :] = (y * _sigmoid(y)).astype(BF16)
    _row_loop(rows, PACKED_ROWS, ln_silu, unroll=4)

    ga_s[...] = _dot(hy_s[...].astype(BF16), w_rnn_ref[...])
    gi_s[...] = _dot(cn_b[...], w_conv_ref[...])

    def merge(r0):
        xn_b[pk(r0), :] = (_sigmoid(grnn_s[pk(r0), :]) * ga_s[pk(r0), :]
                           + _sigmoid(gconv_s[pk(r0), :]) * gi_s[pk(r0), :]).astype(BF16)
    _row_loop(rows, PACKED_ROWS, merge, unroll=2)
    cg_s[...] = _dot(xn_b[...], w_out_ref[...])

    def residual_norm2(r0):
        x1 = x_rows(r0) + cg_s[pk(r0), :]
        x1c_ref[pk(r0), 0:D] = x1
        ms = jnp.mean(x1 * x1, axis=-1, keepdims=True)
        xn2 = x1 * lax.rsqrt(ms + EPS) * vec(V_NORM2)
        hi = xn2.astype(BF16)
        cn_b[pk(r0), :] = hi
        xc_b[pk(r0), :] = (xn2 - hi.astype(F32)).astype(BF16)
    _row_loop(rows, PACKED_ROWS, residual_norm2, unroll=4)

    logits = (_dot(cn_b[...], wr_hi_ref[...]) + _dot(xc_b[...], wr_hi_ref[...])
              + _dot(cn_b[...], wr_lo_ref[...]) + br_ref[...])
    x1c_ref[:, D:D + LANES] = _route_info(logits)

    for k in range(RNN_CONV_W - 1):
        ux_s[k * b_blk:(k + 1) * b_blk, :] = ux_s[(t_tile + k) * b_blk:(t_tile + k + 1) * b_blk, :]
    for k in range(CONF_CONV_W - 1):
        glu_s[k * b_blk:(k + 1) * b_blk, :] = glu_s[(t_tile + k) * b_blk:(t_tile + k + 1) * b_blk, :]

    @pl.when(i == pl.num_programs(1) - 1)
    def _():
        h_out_ref[...] = h_s[...]
        c4_out_ref[...] = ux_s[0:p4, :]
        c31_out_ref[...] = glu_s[0:p31, :]


def _mixer_call(x, h0, c4, c31, wts, *, t_tile, b_blk, batch_major=False):
    if batch_major:
        assert x.shape[0] == b_blk and PACKED_ROWS % b_blk == 0
        n_b, total_rows = 1, x.shape[0] * x.shape[1]
    else:
        n_b, total_rows, _ = x.shape
    rows = t_tile * b_blk
    n_t = total_rows // rows
    assert n_t * rows == total_rows and b_blk % SUBLANES == 0 and rows % PACKED_ROWS == 0
    assert t_tile >= RNN_CONV_W - 1
    p4 = (RNN_CONV_W - 1) * b_blk
    p31 = (CONF_CONV_W - 1) * b_blk

    def tile_spec(width):
        return pl.BlockSpec((None, rows, width), lambda j, i: (j, i, 0))

    def state_spec(n_rows):
        return pl.BlockSpec((None, n_rows, D), lambda j, i: (j, 0, 0))

    def const_spec(arr):
        nd = arr.ndim
        return pl.BlockSpec(arr.shape, lambda j, i: (0,) * nd, pipeline_mode=pl.Buffered(1))

    plane = pltpu.VMEM((rows, D), F32)
    plane_b = pltpu.VMEM((rows, D), BF16)
    return pl.pallas_call(
        functools.partial(_mixer_kernel, t_tile=t_tile, b_blk=b_blk, batch_major=batch_major),
        grid=(n_b, n_t),
        in_specs=[pl.BlockSpec((b_blk, t_tile, D), lambda j, i: (0, i, 0)) if batch_major else tile_spec(D),
                  state_spec(b_blk), state_spec(p4), state_spec(p31)]
                 + [const_spec(w) for w in wts],
        out_specs=[pl.BlockSpec((rows, TOKEN_W), lambda j, i: (j * n_t + i, 0)),
                   state_spec(b_blk), state_spec(p4), state_spec(p31)],
        out_shape=[jax.ShapeDtypeStruct((n_b * total_rows, TOKEN_W), F32),
                   jax.ShapeDtypeStruct((n_b, b_blk, D), F32),
                   jax.ShapeDtypeStruct((n_b, p4, D), F32),
                   jax.ShapeDtypeStruct((n_b, p31, D), F32)],
        scratch_shapes=[plane_b, pltpu.VMEM((p4 + rows, D), F32), plane, plane, plane, plane, plane,
                        pltpu.VMEM((p31 + rows, D), F32), plane, plane_b, plane, plane, plane, plane_b,
                        pltpu.VMEM((b_blk, D), F32)],
        compiler_params=pltpu.CompilerParams(dimension_semantics=("arbitrary", "arbitrary"),
                                             vmem_limit_bytes=VMEM_LIMIT),
        name="mixer",
    )(x, h0, c4, c31, *wts)


def _row_copies(n, make_copy):
    def issue(r, carry):
        make_copy(r).start()
        return carry
    lax.fori_loop(0, n, issue, 0, unroll=8)


def _dispatch_kernel(slot_ref, pad0_ref, padn_ref, tok_a_ref, tok_b_ref, sorted_hbm, zero_s, sem, pad_sem,
                     *, n_a_tiles):
    i = pl.program_id(0)

    def scatter(tok_ref):
        def row_copy(r):
            return pltpu.make_async_copy(tok_ref.at[pl.ds(r, 1), :],
                                         sorted_hbm.at[pl.ds(slot_ref[i * MOE_TILE + r], 1), :], sem)
        _row_copies(MOE_TILE, row_copy)
        pltpu.make_async_copy(tok_ref, sorted_hbm.at[pl.ds(0, MOE_TILE), :], sem).wait()

    @pl.when(i < n_a_tiles)
    def _():
        scatter(tok_a_ref)

    @pl.when(i >= n_a_tiles)
    def _():
        scatter(tok_b_ref)

    @pl.when(i == pl.num_programs(0) - 1)
    def _():
        zero_s[...] = jnp.zeros_like(zero_s)
        for region in range(N_PAD_REGIONS):
            def pad_copy(k, region=region):
                return pltpu.make_async_copy(zero_s.at[pl.ds(0, 1), :],
                                             sorted_hbm.at[pl.ds(pad0_ref[region] + k, 1), :], pad_sem)

            def issue(k, carry):
                pad_copy(k).start()
                return carry

            def drain(k, carry):
                pad_copy(k).wait()
                return carry
            lax.fori_loop(0, padn_ref[region], issue, 0)
            lax.fori_loop(0, padn_ref[region], drain, 0)


def _dispatch_call(tokens_a, tokens_b, slot, pad0, padn, *, n_sorted):
    n_a, n_b = tokens_a.shape[0], tokens_b.shape[0]
    assert n_a % MOE_TILE == 0 and n_b % MOE_TILE == 0
    n_a_tiles = n_a // MOE_TILE
    return pl.pallas_call(
        functools.partial(_dispatch_kernel, n_a_tiles=n_a_tiles),
        grid_spec=pltpu.PrefetchScalarGridSpec(
            num_scalar_prefetch=3, grid=((n_a + n_b) // MOE_TILE,),
            in_specs=[pl.BlockSpec((MOE_TILE, TOKEN_W), lambda i, *_: (jnp.minimum(i, n_a_tiles - 1), 0)),
                      pl.BlockSpec((MOE_TILE, TOKEN_W), lambda i, *_: (jnp.maximum(i - n_a_tiles, 0), 0))],
            out_specs=pl.BlockSpec(memory_space=pl.ANY),
            scratch_shapes=[pltpu.VMEM((SUBLANES, TOKEN_W), F32), pltpu.SemaphoreType.DMA(()),
                            pltpu.SemaphoreType.DMA(())]),
        out_shape=jax.ShapeDtypeStruct((n_sorted, TOKEN_W), F32),
        compiler_params=pltpu.CompilerParams(dimension_semantics=("arbitrary",)),
        name="moe_dispatch",
    )(slot, pad0, padn, tokens_a, tokens_b)


def _moe_kernel(tile_group_ref, n_tiles_ref, tok_ref, wg_ref, wu_ref, wd_ref, gains_ref, y_ref, xn_b, he_b, y_s):
    i = pl.program_id(0)
    live = i < n_tiles_ref[0]

    def pk(r0):
        return pl.ds(r0, PACKED_ROWS)

    @pl.when(live)
    def _():
        def norm2(r0):
            x1 = tok_ref[pk(r0), 0:D]
            ms = jnp.mean(x1 * x1, axis=-1, keepdims=True)
            xn_b[pk(r0), :] = (x1 * lax.rsqrt(ms + EPS) * gains_ref[0:1, :]).astype(BF16)
        _row_loop(MOE_TILE, PACKED_ROWS, norm2, unroll=4)

        info = tok_ref[:, D:D + LANES]
        lane = lax.broadcasted_iota(jnp.int32, info.shape, 1)
        lane0 = EXPERT_LANE0 + tile_group_ref[i] * EXPERTS_PER_GROUP
        for e in range(EXPERTS_PER_GROUP):
            scale = jnp.sum(jnp.where(lane == lane0 + e, info, 0.0), axis=-1, keepdims=True)
            gate = _dot(xn_b[...], wg_ref[e])
            he = gate * _sigmoid(gate) * _dot(xn_b[...], wu_ref[e])
            he_b[:, e * D_EXPERT:(e + 1) * D_EXPERT] = (he * scale).astype(BF16)
        y_s[...] = _dot(he_b[...], wd_ref[...])

        def residual_norm(r0):
            y = tok_ref[pk(r0), 0:D] + y_s[pk(r0), :]
            ms = jnp.mean(y * y, axis=-1, keepdims=True)
            y_ref[pk(r0), :] = y * lax.rsqrt(ms + EPS) * gains_ref[1:2, :]
        _row_loop(MOE_TILE, PACKED_ROWS, residual_norm, unroll=4)

    @pl.when(jnp.logical_not(live))
    def _():
        y_ref[...] = jnp.zeros_like(y_ref)


def _moe_call(sorted_tokens, tile_group, n_tiles, wg, wu, wd, gains):
    n_sorted = sorted_tokens.shape[0]
    assert n_sorted % MOE_TILE == 0
    hidden = EXPERTS_PER_GROUP * D_EXPERT

    def tile(i, tile_group_ref, n_tiles_ref):
        return (jnp.minimum(i, n_tiles_ref[0] - 1), 0)

    def group(i, tile_group_ref, n_tiles_ref):
        return (tile_group_ref[jnp.minimum(i, n_tiles_ref[0] - 1)], 0, 0)

    return pl.pallas_call(
        _moe_kernel,
        grid_spec=pltpu.PrefetchScalarGridSpec(
            num_scalar_prefetch=2, grid=(n_sorted // MOE_TILE,),
            in_specs=[pl.BlockSpec((MOE_TILE, TOKEN_W), tile),
                      pl.BlockSpec((EXPERTS_PER_GROUP, D, D_EXPERT), group),
                      pl.BlockSpec((EXPERTS_PER_GROUP, D, D_EXPERT), group),
                      pl.BlockSpec((None, hidden, D), group),
                      pl.BlockSpec((SUBLANES, D), lambda i, *_: (0, 0))],
            out_specs=pl.BlockSpec((MOE_TILE, D), lambda i, *_: (i, 0)),
            scratch_shapes=[pltpu.VMEM((MOE_TILE, D), BF16), pltpu.VMEM((MOE_TILE, hidden), BF16),
                            pltpu.VMEM((MOE_TILE, D), F32)]),
        out_shape=jax.ShapeDtypeStruct((n_sorted, D), F32),
        compiler_params=pltpu.CompilerParams(dimension_semantics=("arbitrary",),
                                             vmem_limit_bytes=VMEM_LIMIT),
        name="moe_grouped",
    )(tile_group, n_tiles, sorted_tokens, wg, wu, wd, gains)


def _collect_kernel(src_ref, y_hbm, o_ref, sem):
    i = pl.program_id(0)

    def row_copy(r):
        return pltpu.make_async_copy(y_hbm.at[pl.ds(src_ref[i * MOE_TILE + r], 1), :],
                                     o_ref.at[pl.ds(r, 1), :], sem)
    _row_copies(MOE_TILE, row_copy)
    pltpu.make_async_copy(y_hbm.at[pl.ds(0, MOE_TILE), :], o_ref, sem).wait()


def _collect_call(y_sorted, src):
    n = src.shape[0]
    assert n % MOE_TILE == 0
    return pl.pallas_call(
        _collect_kernel,
        grid_spec=pltpu.PrefetchScalarGridSpec(
            num_scalar_prefetch=1, grid=(n // MOE_TILE,),
            in_specs=[pl.BlockSpec(memory_space=pl.ANY)],
            out_specs=pl.BlockSpec((MOE_TILE, D), lambda i, *_: (i, 0)),
            scratch_shapes=[pltpu.SemaphoreType.DMA(())]),
        out_shape=jax.ShapeDtypeStruct((n, D), F32),
        compiler_params=pltpu.CompilerParams(dimension_semantics=("arbitrary",)),
        name="moe_collect",
    )(src, y_sorted)


def _routing_tables(group, n_tile_cap):
    onehot = (group[:, None] == jnp.arange(N_GROUPS, dtype=jnp.int32)[None, :]).astype(jnp.int32)
    running = jnp.cumsum(onehot, axis=0)
    rank = jnp.sum(running * onehot, axis=1) - 1
    counts = running[-1]
    tiles_per_group = (counts + MOE_TILE - 1) // MOE_TILE
    end_tile = jnp.cumsum(tiles_per_group)
    first_row = (end_tile - tiles_per_group) * MOE_TILE
    slot = jnp.sum(first_row[None, :] * onehot, axis=1) + rank
    tile_ids = jnp.arange(n_tile_cap, dtype=jnp.int32)
    tile_group = jnp.minimum(jnp.sum((tile_ids[:, None] >= end_tile[None, :]).astype(jnp.int32), axis=1),
                             N_GROUPS - 1)
    used_rows = end_tile[-1:] * MOE_TILE
    pad0 = jnp.concatenate([first_row + counts, used_rows])
    padn = jnp.concatenate([tiles_per_group * MOE_TILE - counts, n_tile_cap * MOE_TILE - used_rows])
    return (slot.astype(jnp.int32), tile_group.astype(jnp.int32), end_tile[-1:].astype(jnp.int32),
            pad0.astype(jnp.int32), padn.astype(jnp.int32))


def _block_diag_gates(w_a, w_i):
    def bd(w):
        w = w.reshape(N_GATE_BLOCKS, HEADS_PER_BLOCK, HEAD_DIM, HEAD_DIM)
        eye = jnp.eye(HEADS_PER_BLOCK, dtype=w.dtype)
        full = jnp.einsum("bhij,hg->bhigj", w, eye)
        return full.reshape(N_GATE_BLOCKS, MXU_DIM, MXU_DIM)
    return jnp.concatenate([bd(w_a), bd(w_i)], axis=-1).astype(BF16)


def _to_time_major(a, n_b):
    b, t, d = a.shape
    return a.reshape(n_b, b // n_b, t, d).transpose(0, 2, 1, 3).reshape(n_b, t * (b // n_b), d)


def _from_time_major(a, t):
    n_b, rows, d = a.shape
    bb = rows // t
    return a.reshape(n_b, t, bb, d).transpose(0, 2, 1, 3).reshape(n_b * bb, t, d)


def kernel(x_prompt, x_sample, state_rglru_h, state_rglru_conv, state_conformer_conv, meta_tokens, norm1_g, w_in, rnn_conv_w, rnn_conv_b, w_rg_a, b_rg_a, w_rg_i, b_rg_i, rg_lambda, w_rnn_proj, conf_conv_w, conf_conv_b, conf_ln_g, conf_ln_b, w_conv_proj, w_out, norm2_g, w_group, b_group, w_erouter, b_erouter, w_exp_gate, w_exp_up, w_exp_down, final_norm_g):
    assert w_in.shape[0] == 1, "single-layer trunk"
    bsz, seq, _ = x_prompt.shape
    dec_b, dec_t, _ = x_sample.shape

    vec_rows = [norm1_g[0], rnn_conv_b[0], b_rg_a[0], b_rg_i[0], rg_lambda[0], conf_conv_b[0], conf_ln_g[0],
                conf_ln_b[0], norm2_g[0]]
    vec = jnp.zeros((N_VEC_ROWS, D), F32).at[:len(vec_rows)].set(jnp.stack(vec_rows).astype(F32))
    cw4 = jnp.zeros((SUBLANES, D), F32).at[:RNN_CONV_W].set(rnn_conv_w[0])
    cw31 = jnp.zeros((4 * SUBLANES, D), F32).at[:CONF_CONV_W].set(conf_conv_w[0])
    w_router = jnp.zeros((D, LANES), F32)
    w_router = w_router.at[:, :N_GROUPS].set(w_group[0]).at[:, N_GROUPS:N_GROUPS + N_EXPERTS].set(w_erouter[0])
    wr_hi = w_router.astype(BF16)
    wr_lo = (w_router - wr_hi.astype(F32)).astype(BF16)
    b_router = jnp.zeros((1, LANES), F32)
    b_router = b_router.at[0, :N_GROUPS].set(b_group[0]).at[0, N_GROUPS:N_GROUPS + N_EXPERTS].set(
        b_erouter[0].reshape(-1))
    wts = (vec, cw4, cw31, w_in[0].astype(BF16), _block_diag_gates(w_rg_a[0], w_rg_i[0]),
           w_rnn_proj[0].astype(BF16), w_conv_proj[0].astype(BF16), w_out[0].astype(BF16), wr_hi, wr_lo, b_router)
    wg = w_exp_gate[0].astype(BF16)
    wu = w_exp_up[0].astype(BF16)
    wd = w_exp_down[0].astype(BF16).reshape(N_GROUPS, EXPERTS_PER_GROUP * D_EXPERT, D)
    gains = jnp.zeros((SUBLANES, D), F32).at[0].set(norm2_g[0]).at[1].set(final_norm_g)

    n_p = bsz * seq
    n_s = dec_b * dec_t

    meta_tm = jnp.broadcast_to(meta_tokens[:, None, :], (N_META, bsz, D)).reshape(1, N_META * bsz, D)
    _, h_m, c4_m, c31_m = _mixer_call(
        meta_tm, jnp.zeros((1, bsz, D), F32), jnp.zeros((1, (RNN_CONV_W - 1) * bsz, D), F32),
        jnp.zeros((1, (CONF_CONV_W - 1) * bsz, D), F32), wts, t_tile=N_META, b_blk=bsz)

    tok_p, h_p, c4_p, c31_p = _mixer_call(
        x_prompt, h_m, c4_m, c31_m, wts, t_tile=32, b_blk=bsz, batch_major=True)

    n_b = 4
    bb = dec_b // n_b
    tok_s, h_s, c4_s, c31_s = _mixer_call(
        _to_time_major(x_sample, n_b), state_rglru_h[0].reshape(n_b, bb, D),
        _to_time_major(state_rglru_conv[0], n_b), _to_time_major(state_conformer_conv[0], n_b), wts,
        t_tile=dec_t, b_blk=bb)

    n_tile_cap = (n_p + n_s) // MOE_TILE + N_GROUPS
    group = jnp.concatenate([tok_p[:, D], tok_s[:, D]]).astype(jnp.int32)
    slot, tile_group, n_tiles, pad0, padn = _routing_tables(group, n_tile_cap)
    sorted_tokens = _dispatch_call(tok_p, tok_s, slot, pad0, padn, n_sorted=n_tile_cap * MOE_TILE)
    y_sorted = _moe_call(sorted_tokens, tile_group, n_tiles, wg, wu, wd, gains)
    src_p = slot[:n_p].reshape(seq, bsz).T.reshape(-1)
    src_s = slot[n_p:].reshape(n_b, dec_t, bb).transpose(0, 2, 1).reshape(-1)
    y_prompt = _collect_call(y_sorted, src_p).reshape(bsz, seq, D)
    y_sample = _collect_call(y_sorted, src_s).reshape(dec_b, dec_t, D)

    return (y_prompt, y_sample,
            h_p.reshape(1, bsz, D), _from_time_major(c4_p, RNN_CONV_W - 1)[None],
            _from_time_major(c31_p, CONF_CONV_W - 1)[None],
            h_s.reshape(1, dec_b, D), _from_time_major(c4_s, RNN_CONV_W - 1)[None],
            _from_time_major(c31_s, CONF_CONV_W - 1)[None])
```

```python
import functools

import jax
import jax.numpy as jnp
from jax import lax
from jax.experimental import pallas as pl
from jax.experimental.pallas import tpu as pltpu

D = 1024
N_META = 16
N_HEADS = 16
HEAD_DIM = D // N_HEADS
RNN_CONV_W = 4
CONF_CONV_W = 31
LRU_C = 8.0
N_GROUPS = 4
EXPERTS_PER_GROUP = 8
N_EXPERTS = N_GROUPS * EXPERTS_PER_GROUP
D_EXPERT = D // 4
EPS = 1e-6

SUBLANES = 8
PACKED_ROWS = 16
LANES = 128
N_COLS = D // LANES
MXU_DIM = 256
HEADS_PER_BLOCK = MXU_DIM // HEAD_DIM
N_GATE_BLOCKS = D // MXU_DIM
VMEM_LIMIT = 56 * 1024 * 1024
TOKEN_W = D + LANES
EXPERT_LANE0 = N_GROUPS
MOE_TILE = 256
COPY_TILE = 512
N_PAD_REGIONS = N_GROUPS + 1

F32 = jnp.float32
BF16 = jnp.bfloat16

(V_NORM1, V_CONV4_B, V_BA, V_BI, V_LAMBDA, V_CONV31_B, V_LN_G, V_LN_B, V_NORM2) = range(9)
N_VEC_ROWS = 16


def _dot(a, b):
    return jnp.dot(a, b, preferred_element_type=F32)


def _sigmoid(x):
    return 0.5 * jnp.tanh(0.5 * x) + 0.5


def _row_loop(n_rows, chunk, body, unroll):
    n = n_rows // chunk
    assert n * chunk == n_rows

    def wrapped(s, carry):
        body(pl.multiple_of(s * chunk, chunk))
        return carry
    lax.fori_loop(0, n, wrapped, 0, unroll=min(unroll, n))


def _route_info(lg):
    neg = float("-inf")
    lane = lax.broadcasted_iota(jnp.int32, lg.shape, 1).astype(F32)
    no_lane = float(LANES)

    def first_lane_of_max(vals, vmax):
        return jnp.min(jnp.where(vals == vmax, lane, no_lane), axis=-1, keepdims=True)

    gmask = lane < N_GROUPS
    gl = jnp.where(gmask, lg, neg)
    gmax = jnp.max(gl, axis=-1, keepdims=True)
    gidx = first_lane_of_max(gl, gmax)
    gsum = jnp.sum(jnp.where(gmask, jnp.exp(gl - gmax), 0.0), axis=-1, keepdims=True)
    g_gate = 1.0 / gsum
    first = EXPERT_LANE0 + gidx * EXPERTS_PER_GROUP
    emask = (lane >= first) & (lane < first + EXPERTS_PER_GROUP)
    el = jnp.where(emask, lg, neg)
    v1 = jnp.max(el, axis=-1, keepdims=True)
    i1 = first_lane_of_max(el, v1)
    el2 = jnp.where(lane == i1, neg, el)
    v2 = jnp.max(el2, axis=-1, keepdims=True)
    i2 = first_lane_of_max(el2, v2)
    e2 = jnp.exp(v2 - v1)
    w1 = 1.0 / (1.0 + e2)
    w2 = e2 / (1.0 + e2)
    comb = jnp.where(lane == i1, w1 * g_gate, 0.0) + jnp.where(lane == i2, w2 * g_gate, 0.0)
    return jnp.where(lane == 0.0, gidx, comb)


def _mixer_kernel(x_ref, h0_ref, c4_ref, c31_ref, vec_ref, cw4_ref, cw31_ref,
                  w_in_ref, w_gate_ref, w_rnn_ref, w_conv_ref, w_out_ref, wr_hi_ref, wr_lo_ref, br_ref,
                  x1c_ref, h_out_ref, c4_out_ref, c31_out_ref,
                  xn_b, ux_s, uy_s, cv_s, cg_s, grnn_s, gconv_s, glu_s, xc_s, xc_b, ga_s, gi_s, hy_s, cn_b, h_s,
                  *, t_tile, b_blk, batch_major):
    i = pl.program_id(1)
    rows = t_tile * b_blk
    p4 = (RNN_CONV_W - 1) * b_blk
    p31 = (CONF_CONV_W - 1) * b_blk

    @pl.when(i == 0)
    def _():
        h_s[...] = h0_ref[...]
        ux_s[0:p4, :] = c4_ref[...]
        glu_s[0:p31, :] = c31_ref[...]

    def pk(r0):
        return pl.ds(r0, PACKED_ROWS)

    def vec(row):
        return vec_ref[row:row + 1, :]

    def x_rows(r0):
        if not batch_major:
            return x_ref[pk(r0), :]
        t = r0 // b_blk
        return jnp.concatenate([x_ref[:, t + k, :] for k in range(PACKED_ROWS // b_blk)], axis=0)

    def norm1(r0):
        x = x_rows(r0)
        ms = jnp.mean(x * x, axis=-1, keepdims=True)
        xn_b[pk(r0), :] = (x * lax.rsqrt(ms + EPS) * vec(V_NORM1)).astype(BF16)
    _row_loop(rows, PACKED_ROWS, norm1, unroll=4)

    def in_proj(group):
        return _dot(xn_b[...], w_in_ref[:, group * D:(group + 1) * D])
    cv_s[...] = in_proj(2)
    cg_s[...] = in_proj(3)

    def glu(r0):
        glu_s[pk(p31 + r0), :] = cv_s[pk(r0), :] * _sigmoid(cg_s[pk(r0), :])
    _row_loop(rows, PACKED_ROWS, glu, unroll=4)

    strip_t = min(SUBLANES, t_tile)
    n_strip_t = t_tile // strip_t
    assert n_strip_t * strip_t == t_tile

    def conv31_strip_col(r0, col):
        cs = slice(col * LANES, (col + 1) * LANES)
        accs = [None] * strip_t
        for k in range(CONF_CONV_W):
            w = cw31_ref[k:k + 1, cs]
            for o in range(strip_t):
                term = glu_s[r0 + (o + k) * b_blk:r0 + (o + k) * b_blk + SUBLANES, cs] * w
                accs[o] = term if k == 0 else accs[o] + term
        bias = vec_ref[V_CONV31_B:V_CONV31_B + 1, cs]
        for o in range(strip_t):
            cv_s[r0 + o * b_blk:r0 + o * b_blk + SUBLANES, cs] = accs[o] + bias

    pieces = [(plane, row0, group * D + c0, c0)
              for plane, row0, group in ((ux_s, p4, 0), (uy_s, 0, 1), (grnn_s, 0, 4), (gconv_s, 0, 5))
              for c0 in range(0, D, MXU_DIM)]

    def in_proj_piece(plane, row0, w_col, c0):
        plane[row0:row0 + rows, c0:c0 + MXU_DIM] = _dot(xn_b[...], w_in_ref[:, w_col:w_col + MXU_DIM])

    strip_cols = [(tc * strip_t * b_blk + jb * SUBLANES, col)
                  for jb in range(b_blk // SUBLANES) for tc in range(n_strip_t) for col in range(N_COLS)]
    assert len(strip_cols) % len(pieces) == 0
    cols_per_piece = len(strip_cols) // len(pieces)
    for q, piece in enumerate(pieces):
        in_proj_piece(*piece)
        for r0, col in strip_cols[q * cols_per_piece:(q + 1) * cols_per_piece]:
            conv31_strip_col(r0, col)

    def conv4(r0):
        acc = ux_s[pk(r0), :] * cw4_ref[0:1, :]
        for k in range(1, RNN_CONV_W):
            acc = acc + ux_s[pk(r0 + k * b_blk), :] * cw4_ref[k:k + 1, :]
        xc = acc + vec(V_CONV4_B)
        xc_s[pk(r0), :] = xc
        xc_b[pk(r0), :] = xc.astype(BF16)
    _row_loop(rows, PACKED_ROWS, conv4, unroll=2)

    for blk in range(N_GATE_BLOCKS):
        cols = slice(blk * MXU_DIM, (blk + 1) * MXU_DIM)
        g = _dot(xc_b[:, cols], w_gate_ref[blk])
        ga_s[:, cols] = g[:, :MXU_DIM]
        gi_s[:, cols] = g[:, MXU_DIM:]

    c_lam = -LRU_C * jax.nn.softplus(-vec(V_LAMBDA))

    def lru_coeffs(r0):
        r = _sigmoid(ga_s[pk(r0), :] + vec(V_BA))
        ig = _sigmoid(gi_s[pk(r0), :] + vec(V_BI))
        log_a = c_lam * r
        a = jnp.exp(log_a)
        mult = jnp.sqrt(-jnp.tanh(log_a) * (a * a + 1.0))
        ga_s[pk(r0), :] = a
        gi_s[pk(r0), :] = mult * (ig * xc_s[pk(r0), :])
        uy_s[pk(r0), :] = jax.nn.gelu(uy_s[pk(r0), :])
    _row_loop(rows, PACKED_ROWS, lru_coeffs, unroll=2)

    def scan_batch_slab(jb, carry):
        b0 = pl.multiple_of(jb * SUBLANES, SUBLANES)

        def step(t, h):
            sl = pl.ds(pl.multiple_of(t * b_blk + b0, SUBLANES), SUBLANES)
            h = ga_s[sl, :] * h + gi_s[sl, :]
            hy_s[sl, :] = h * uy_s[sl, :]
            return h

        h_s[pl.ds(b0, SUBLANES), :] = lax.fori_loop(0, t_tile, step, h_s[pl.ds(b0, SUBLANES), :],
                                                    unroll=min(8, t_tile))
        return carry
    lax.fori_loop(0, b_blk // SUBLANES, scan_batch_slab, 0)

    def ln_silu(r0):
        cc = cv_s[pk(r0), :]
        mu = jnp.mean(cc, axis=-1, keepdims=True)
        cen = cc - mu
        var = jnp.mean(cen * cen, axis=-1, keepdims=True)
        y = cen * lax.rsqrt(var + EPS) * vec(V_LN_G) + vec(V_LN_B)
        cn_b[pk(r0), :] = (y * _sigmoid(y)).astype(BF16)
    _row_loop(rows, PACKED_ROWS, ln_silu, unroll=4)

    ga_s[...] = _dot(hy_s[...].astype(BF16), w_rnn_ref[...])
    gi_s[...] = _dot(cn_b[...], w_conv_ref[...])

    def merge(r0):
        xn_b[pk(r0), :] = (_sigmoid(grnn_s[pk(r0), :]) * ga_s[pk(r0), :]
                           + _sigmoid(gconv_s[pk(r0), :]) * gi_s[pk(r0), :]).astype(BF16)
    _row_loop(rows, PACKED_ROWS, merge, unroll=2)
    cg_s[...] = _dot(xn_b[...], w_out_ref[...])

    def residual_norm2(r0):
        x1 = x_rows(r0) + cg_s[pk(r0), :]
        x1c_ref[pk(r0), 0:D] = x1
        ms = jnp.mean(x1 * x1, axis=-1, keepdims=True)
        xn2 = x1 * lax.rsqrt(ms + EPS) * vec(V_NORM2)
        hi = xn2.astype(BF16)
        cn_b[pk(r0), :] = hi
        xc_b[pk(r0), :] = (xn2 - hi.astype(F32)).astype(BF16)
    _row_loop(rows, PACKED_ROWS, residual_norm2, unroll=4)

    logits = (_dot(cn_b[...], wr_hi_ref[...]) + _dot(xc_b[...], wr_hi_ref[...])
              + _dot(cn_b[...], wr_lo_ref[...]) + br_ref[...])
    x1c_ref[:, D:D + LANES] = _route_info(logits)

    for k in range(RNN_CONV_W - 1):
        ux_s[k * b_blk:(k + 1) * b_blk, :] = ux_s[(t_tile + k) * b_blk:(t_tile + k + 1) * b_blk, :]
    for k in range(CONF_CONV_W - 1):
        glu_s[k * b_blk:(k + 1) * b_blk, :] = glu_s[(t_tile + k) * b_blk:(t_tile + k + 1) * b_blk, :]

    @pl.when(i == pl.num_programs(1) - 1)
    def _():
        h_out_ref[...] = h_s[...]
        c4_out_ref[...] = ux_s[0:p4, :]
        c31_out_ref[...] = glu_s[0:p31, :]


def _mixer_call(x, h0, c4, c31, wts, *, t_tile, b_blk, batch_major=False):
    if batch_major:
        assert x.shape[0] == b_blk and PACKED_ROWS % b_blk == 0
        n_b, total_rows = 1, x.shape[0] * x.shape[1]
    else:
        n_b, total_rows, _ = x.shape
    rows = t_tile * b_blk
    n_t = total_rows // rows
    assert n_t * rows == total_rows and b_blk % SUBLANES == 0 and rows % PACKED_ROWS == 0
    assert t_tile >= RNN_CONV_W - 1
    p4 = (RNN_CONV_W - 1) * b_blk
    p31 = (CONF_CONV_W - 1) * b_blk

    def tile_spec(width):
        return pl.BlockSpec((None, rows, width), lambda j, i: (j, i, 0))

    def state_spec(n_rows):
        return pl.BlockSpec((None, n_rows, D), lambda j, i: (j, 0, 0))

    def const_spec(arr):
        nd = arr.ndim
        return pl.BlockSpec(arr.shape, lambda j, i: (0,) * nd, pipeline_mode=pl.Buffered(1))

    plane = pltpu.VMEM((rows, D), F32)
    plane_b = pltpu.VMEM((rows, D), BF16)
    return pl.pallas_call(
        functools.partial(_mixer_kernel, t_tile=t_tile, b_blk=b_blk, batch_major=batch_major),
        grid=(n_b, n_t),
        in_specs=[pl.BlockSpec((b_blk, t_tile, D), lambda j, i: (0, i, 0)) if batch_major else tile_spec(D),
                  state_spec(b_blk), state_spec(p4), state_spec(p31)]
                 + [const_spec(w) for w in wts],
        out_specs=[pl.BlockSpec((rows, TOKEN_W), lambda j, i: (j * n_t + i, 0)),
                   state_spec(b_blk), state_spec(p4), state_spec(p31)],
        out_shape=[jax.ShapeDtypeStruct((n_b * total_rows, TOKEN_W), F32),
                   jax.ShapeDtypeStruct((n_b, b_blk, D), F32),
                   jax.ShapeDtypeStruct((n_b, p4, D), F32),
                   jax.ShapeDtypeStruct((n_b, p31, D), F32)],
        scratch_shapes=[plane_b, pltpu.VMEM((p4 + rows, D), F32), plane, plane, plane, plane, plane,
                        pltpu.VMEM((p31 + rows, D), F32), plane, plane_b, plane, plane, plane, plane_b,
                        pltpu.VMEM((b_blk, D), F32)],
        compiler_params=pltpu.CompilerParams(dimension_semantics=("arbitrary", "arbitrary"),
                                             vmem_limit_bytes=VMEM_LIMIT),
        name="mixer",
    )(x, h0, c4, c31, *wts)


def _row_copies(n, make_copy):
    def issue(r, carry):
        make_copy(r).start()
        return carry
    lax.fori_loop(0, n, issue, 0, unroll=8)


def _dispatch_kernel(slot_ref, pad0_ref, padn_ref, tok_a_ref, tok_b_ref, sorted_hbm, zero_s, sem, pad_sem,
                     *, n_a_tiles):
    i = pl.program_id(0)

    def scatter(tok_ref):
        def row_copy(r):
            return pltpu.make_async_copy(tok_ref.at[pl.ds(r, 1), :],
                                         sorted_hbm.at[pl.ds(slot_ref[i * COPY_TILE + r], 1), :], sem)
        _row_copies(COPY_TILE, row_copy)
        pltpu.make_async_copy(tok_ref, sorted_hbm.at[pl.ds(0, COPY_TILE), :], sem).wait()

    @pl.when(i < n_a_tiles)
    def _():
        scatter(tok_a_ref)

    @pl.when(i >= n_a_tiles)
    def _():
        scatter(tok_b_ref)

    @pl.when(i == pl.num_programs(0) - 1)
    def _():
        zero_s[...] = jnp.zeros_like(zero_s)
        for region in range(N_PAD_REGIONS):
            def pad_copy(k, region=region):
                return pltpu.make_async_copy(zero_s.at[pl.ds(0, 1), :],
                                             sorted_hbm.at[pl.ds(pad0_ref[region] + k, 1), :], pad_sem)

            def issue(k, carry):
                pad_copy(k).start()
                return carry

            def drain(k, carry):
                pad_copy(k).wait()
                return carry
            lax.fori_loop(0, padn_ref[region], issue, 0)
            lax.fori_loop(0, padn_ref[region], drain, 0)


def _dispatch_call(tokens_a, tokens_b, slot, pad0, padn, *, n_sorted):
    n_a, n_b = tokens_a.shape[0], tokens_b.shape[0]
    assert n_a % COPY_TILE == 0 and n_b % COPY_TILE == 0
    n_a_tiles = n_a // COPY_TILE
    return pl.pallas_call(
        functools.partial(_dispatch_kernel, n_a_tiles=n_a_tiles),
        grid_spec=pltpu.PrefetchScalarGridSpec(
            num_scalar_prefetch=3, grid=((n_a + n_b) // COPY_TILE,),
            in_specs=[pl.BlockSpec((COPY_TILE, TOKEN_W), lambda i, *_: (jnp.minimum(i, n_a_tiles - 1), 0)),
                      pl.BlockSpec((COPY_TILE, TOKEN_W), lambda i, *_: (jnp.maximum(i - n_a_tiles, 0), 0))],
            out_specs=pl.BlockSpec(memory_space=pl.ANY),
            scratch_shapes=[pltpu.VMEM((SUBLANES, TOKEN_W), F32), pltpu.SemaphoreType.DMA(()),
                            pltpu.SemaphoreType.DMA(())]),
        out_shape=jax.ShapeDtypeStruct((n_sorted, TOKEN_W), F32),
        compiler_params=pltpu.CompilerParams(dimension_semantics=("arbitrary",)),
        name="moe_dispatch",
    )(slot, pad0, padn, tokens_a, tokens_b)


def _moe_kernel(tile_group_ref, n_tiles_ref, tok_ref, wg_ref, wu_ref, wd_ref, gains_ref, y_ref, xn_b, he_b, y_s):
    i = pl.program_id(0)
    live = i < n_tiles_ref[0]

    def pk(r0):
        return pl.ds(r0, PACKED_ROWS)

    @pl.when(live)
    def _():
        def norm2(r0):
            x1 = tok_ref[pk(r0), 0:D]
            ms = jnp.mean(x1 * x1, axis=-1, keepdims=True)
            xn_b[pk(r0), :] = (x1 * lax.rsqrt(ms + EPS) * gains_ref[0:1, :]).astype(BF16)
        _row_loop(MOE_TILE, PACKED_ROWS, norm2, unroll=4)

        info = tok_ref[:, D:D + LANES]
        lane = lax.broadcasted_iota(jnp.int32, info.shape, 1)
        lane0 = EXPERT_LANE0 + tile_group_ref[i] * EXPERTS_PER_GROUP
        for e in range(EXPERTS_PER_GROUP):
            scale = jnp.sum(jnp.where(lane == lane0 + e, info, 0.0), axis=-1, keepdims=True)
            gate = _dot(xn_b[...], wg_ref[e])
            he = gate * _sigmoid(gate) * _dot(xn_b[...], wu_ref[e])
            he_b[:, e * D_EXPERT:(e + 1) * D_EXPERT] = (he * scale).astype(BF16)
        y_s[...] = _dot(he_b[...], wd_ref[...])

        def residual_norm(r0):
            y = tok_ref[pk(r0), 0:D] + y_s[pk(r0), :]
            ms = jnp.mean(y * y, axis=-1, keepdims=True)
            y_ref[pk(r0), :] = y * lax.rsqrt(ms + EPS) * gains_ref[1:2, :]
        _row_loop(MOE_TILE, PACKED_ROWS, residual_norm, unroll=4)

    @pl.when(jnp.logical_not(live))
    def _():
        y_ref[...] = jnp.zeros_like(y_ref)


def _moe_call(sorted_tokens, tile_group, n_tiles, wg, wu, wd, gains):
    n_sorted = sorted_tokens.shape[0]
    assert n_sorted % MOE_TILE == 0
    hidden = EXPERTS_PER_GROUP * D_EXPERT

    def tile(i, tile_group_ref, n_tiles_ref):
        return (jnp.minimum(i, n_tiles_ref[0] - 1), 0)

    def group(i, tile_group_ref, n_tiles_ref):
        return (tile_group_ref[jnp.minimum(i, n_tiles_ref[0] - 1)], 0, 0)

    return pl.pallas_call(
        _moe_kernel,
        grid_spec=pltpu.PrefetchScalarGridSpec(
            num_scalar_prefetch=2, grid=(n_sorted // MOE_TILE,),
            in_specs=[pl.BlockSpec((MOE_TILE, TOKEN_W), tile),
                      pl.BlockSpec((EXPERTS_PER_GROUP, D, D_EXPERT), group),
                      pl.BlockSpec((EXPERTS_PER_GROUP, D, D_EXPERT), group),
                      pl.BlockSpec((None, hidden, D), group),
                      pl.BlockSpec((SUBLANES, D), lambda i, *_: (0, 0))],
            out_specs=pl.BlockSpec((MOE_TILE, D), lambda i, *_: (i, 0)),
            scratch_shapes=[pltpu.VMEM((MOE_TILE, D), BF16), pltpu.VMEM((MOE_TILE, hidden), BF16),
                            pltpu.VMEM((MOE_TILE, D), F32)]),
        out_shape=jax.ShapeDtypeStruct((n_sorted, D), F32),
        compiler_params=pltpu.CompilerParams(dimension_semantics=("arbitrary",),
                                             vmem_limit_bytes=VMEM_LIMIT),
        name="moe_grouped",
    )(tile_group, n_tiles, sorted_tokens, wg, wu, wd, gains)


def _collect_kernel(src_ref, y_hbm, o_ref, sem):
    i = pl.program_id(0)

    def row_copy(r):
        return pltpu.make_async_copy(y_hbm.at[pl.ds(src_ref[i * COPY_TILE + r], 1), :],
                                     o_ref.at[pl.ds(r, 1), :], sem)
    _row_copies(COPY_TILE, row_copy)
    pltpu.make_async_copy(y_hbm.at[pl.ds(0, COPY_TILE), :], o_ref, sem).wait()


def _collect_call(y_sorted, src):
    n = src.shape[0]
    assert n % COPY_TILE == 0
    return pl.pallas_call(
        _collect_kernel,
        grid_spec=pltpu.PrefetchScalarGridSpec(
            num_scalar_prefetch=1, grid=(n // COPY_TILE,),
            in_specs=[pl.BlockSpec(memory_space=pl.ANY)],
            out_specs=pl.BlockSpec((COPY_TILE, D), lambda i, *_: (i, 0)),
            scratch_shapes=[pltpu.SemaphoreType.DMA(())]),
        out_shape=jax.ShapeDtypeStruct((n, D), F32),
        compiler_params=pltpu.CompilerParams(dimension_semantics=("arbitrary",)),
        name="moe_collect",
    )(src, y_sorted)


def _routing_tables(group, n_tile_cap):
    onehot = (group[:, None] == jnp.arange(N_GROUPS, dtype=jnp.int32)[None, :]).astype(jnp.int32)
    running = jnp.cumsum(onehot, axis=0)
    rank = jnp.sum(running * onehot, axis=1) - 1
    counts = running[-1]
    tiles_per_group = (counts + MOE_TILE - 1) // MOE_TILE
    end_tile = jnp.cumsum(tiles_per_group)
    first_row = (end_tile - tiles_per_group) * MOE_TILE
    slot = jnp.sum(first_row[None, :] * onehot, axis=1) + rank
    tile_ids = jnp.arange(n_tile_cap, dtype=jnp.int32)
    tile_group = jnp.minimum(jnp.sum((tile_ids[:, None] >= end_tile[None, :]).astype(jnp.int32), axis=1),
                             N_GROUPS - 1)
    used_rows = end_tile[-1:] * MOE_TILE
    pad0 = jnp.concatenate([first_row + counts, used_rows])
    padn = jnp.concatenate([tiles_per_group * MOE_TILE - counts, n_tile_cap * MOE_TILE - used_rows])
    return (slot.astype(jnp.int32), tile_group.astype(jnp.int32), end_tile[-1:].astype(jnp.int32),
            pad0.astype(jnp.int32), padn.astype(jnp.int32))


def _block_diag_gates(w_a, w_i):
    def bd(w):
        w = w.reshape(N_GATE_BLOCKS, HEADS_PER_BLOCK, HEAD_DIM, HEAD_DIM)
        eye = jnp.eye(HEADS_PER_BLOCK, dtype=w.dtype)
        full = jnp.einsum("bhij,hg->bhigj", w, eye)
        return full.reshape(N_GATE_BLOCKS, MXU_DIM, MXU_DIM)
    return jnp.concatenate([bd(w_a), bd(w_i)], axis=-1).astype(BF16)


def _to_time_major(a, n_b):
    b, t, d = a.shape
    return a.reshape(n_b, b // n_b, t, d).transpose(0, 2, 1, 3).reshape(n_b, t * (b // n_b), d)


def _from_time_major(a, t):
    n_b, rows, d = a.shape
    bb = rows // t
    return a.reshape(n_b, t, bb, d).transpose(0, 2, 1, 3).reshape(n_b * bb, t, d)


def kernel(x_prompt, x_sample, state_rglru_h, state_rglru_conv, state_conformer_conv, meta_tokens, norm1_g, w_in, rnn_conv_w, rnn_conv_b, w_rg_a, b_rg_a, w_rg_i, b_rg_i, rg_lambda, w_rnn_proj, conf_conv_w, conf_conv_b, conf_ln_g, conf_ln_b, w_conv_proj, w_out, norm2_g, w_group, b_group, w_erouter, b_erouter, w_exp_gate, w_exp_up, w_exp_down, final_norm_g):
    assert w_in.shape[0] == 1, "single-layer trunk"
    bsz, seq, _ = x_prompt.shape
    dec_b, dec_t, _ = x_sample.shape

    vec_rows = [norm1_g[0], rnn_conv_b[0], b_rg_a[0], b_rg_i[0], rg_lambda[0], conf_conv_b[0], conf_ln_g[0],
                conf_ln_b[0], norm2_g[0]]
    vec = jnp.zeros((N_VEC_ROWS, D), F32).at[:len(vec_rows)].set(jnp.stack(vec_rows).astype(F32))
    cw4 = jnp.zeros((SUBLANES, D), F32).at[:RNN_CONV_W].set(rnn_conv_w[0])
    cw31 = jnp.zeros((4 * SUBLANES, D), F32).at[:CONF_CONV_W].set(conf_conv_w[0])
    w_router = jnp.zeros((D, LANES), F32)
    w_router = w_router.at[:, :N_GROUPS].set(w_group[0]).at[:, N_GROUPS:N_GROUPS + N_EXPERTS].set(w_erouter[0])
    wr_hi = w_router.astype(BF16)
    wr_lo = (w_router - wr_hi.astype(F32)).astype(BF16)
    b_router = jnp.zeros((1, LANES), F32)
    b_router = b_router.at[0, :N_GROUPS].set(b_group[0]).at[0, N_GROUPS:N_GROUPS + N_EXPERTS].set(
        b_erouter[0].reshape(-1))
    wts = (vec, cw4, cw31, w_in[0].astype(BF16), _block_diag_gates(w_rg_a[0], w_rg_i[0]),
           w_rnn_proj[0].astype(BF16), w_conv_proj[0].astype(BF16), w_out[0].astype(BF16), wr_hi, wr_lo, b_router)
    wg = w_exp_gate[0].astype(BF16)
    wu = w_exp_up[0].astype(BF16)
    wd = w_exp_down[0].astype(BF16).reshape(N_GROUPS, EXPERTS_PER_GROUP * D_EXPERT, D)
    gains = jnp.zeros((SUBLANES, D), F32).at[0].set(norm2_g[0]).at[1].set(final_norm_g)

    n_p = bsz * seq
    n_s = dec_b * dec_t

    meta_tm = jnp.broadcast_to(meta_tokens[:, None, :], (N_META, bsz, D)).reshape(1, N_META * bsz, D)
    _, h_m, c4_m, c31_m = _mixer_call(
        meta_tm, jnp.zeros((1, bsz, D), F32), jnp.zeros((1, (RNN_CONV_W - 1) * bsz, D), F32),
        jnp.zeros((1, (CONF_CONV_W - 1) * bsz, D), F32), wts, t_tile=N_META, b_blk=bsz)

    tok_p, h_p, c4_p, c31_p = _mixer_call(
        x_prompt, h_m, c4_m, c31_m, wts, t_tile=32, b_blk=bsz, batch_major=True)

    n_b = 4
    bb = dec_b // n_b
    tok_s, h_s, c4_s, c31_s = _mixer_call(
        _to_time_major(x_sample, n_b), state_rglru_h[0].reshape(n_b, bb, D),
        _to_time_major(state_rglru_conv[0], n_b), _to_time_major(state_conformer_conv[0], n_b), wts,
        t_tile=dec_t, b_blk=bb)

    n_tile_cap = (n_p + n_s) // MOE_TILE + N_GROUPS
    group = jnp.concatenate([tok_p[:, D], tok_s[:, D]]).astype(jnp.int32)
    slot, tile_group, n_tiles, pad0, padn = _routing_tables(group, n_tile_cap)
    sorted_tokens = _dispatch_call(tok_p, tok_s, slot, pad0, padn, n_sorted=n_tile_cap * MOE_TILE)
    y_sorted = _moe_call(sorted_tokens, tile_group, n_tiles, wg, wu, wd, gains)
    src_p = slot[:n_p].reshape(seq, bsz).T.reshape(-1)
    src_s = slot[n_p:].reshape(n_b, dec_t, bb).transpose(0, 2, 1).reshape(-1)
    y_prompt = _collect_call(y_sorted, src_p).reshape(bsz, seq, D)
    y_sample = _collect_call(y_sorted, src_s).reshape(dec_b, dec_t, D)

    return (y_prompt, y_sample,
            h_p.reshape(1, bsz, D), _from_time_major(c4_p, RNN_CONV_W - 1)[None],
            _from_time_major(c31_p, CONF_CONV_W - 1)[None],
            h_s.reshape(1, dec_b, D), _from_time_major(c4_s, RNN_CONV_W - 1)[None],
            _from_time_major(c31_s, CONF_CONV_W - 1)[None])
```

```python
import functools

import jax
import jax.numpy as jnp
from jax import lax
from jax.experimental import pallas as pl
from jax.experimental.pallas import tpu as pltpu

D = 1024
N_META = 16
N_HEADS = 16
HEAD_DIM = D // N_HEADS
RNN_CONV_W = 4
CONF_CONV_W = 31
LRU_C = 8.0
N_GROUPS = 4
EXPERTS_PER_GROUP = 8
N_EXPERTS = N_GROUPS * EXPERTS_PER_GROUP
D_EXPERT = D // 4
EPS = 1e-6

SUBLANES = 8
PACKED_ROWS = 16
LANES = 128
N_COLS = D // LANES
MXU_DIM = 256
HEADS_PER_BLOCK = MXU_DIM // HEAD_DIM
N_GATE_BLOCKS = D // MXU_DIM
VMEM_LIMIT = 56 * 1024 * 1024
TOKEN_W = D + LANES
EXPERT_LANE0 = N_GROUPS
MOE_TILE = 256
COPY_TILE = 512
STATS_UNROLL = 16
N_PAD_REGIONS = N_GROUPS + 1

F32 = jnp.float32
BF16 = jnp.bfloat16

(V_NORM1, V_CONV4_B, V_BA, V_BI, V_LAMBDA, V_CONV31_B, V_LN_G, V_LN_B, V_NORM2) = range(9)
N_VEC_ROWS = 16


def _dot(a, b):
    return jnp.dot(a, b, preferred_element_type=F32)


def _sigmoid(x):
    return 0.5 * jnp.tanh(0.5 * x) + 0.5


def _on_lanes(col):
    return jnp.broadcast_to(col, (col.shape[0], LANES))


def _across(tile):
    return jnp.concatenate([tile] * N_COLS, axis=1)


def _row_loop(n_rows, chunk, body, unroll):
    n = n_rows // chunk
    assert n * chunk == n_rows

    def wrapped(s, carry):
        body(pl.multiple_of(s * chunk, chunk))
        return carry
    lax.fori_loop(0, n, wrapped, 0, unroll=min(unroll, n))


def _route_info(lg):
    neg = float("-inf")
    lane = lax.broadcasted_iota(jnp.int32, lg.shape, 1).astype(F32)
    no_lane = float(LANES)

    def first_lane_of_max(vals, vmax):
        return jnp.min(jnp.where(vals == vmax, lane, no_lane), axis=-1, keepdims=True)

    gmask = lane < N_GROUPS
    gl = jnp.where(gmask, lg, neg)
    gmax = jnp.max(gl, axis=-1, keepdims=True)
    gidx = first_lane_of_max(gl, gmax)
    gsum = jnp.sum(jnp.where(gmask, jnp.exp(gl - gmax), 0.0), axis=-1, keepdims=True)
    g_gate = 1.0 / gsum
    first = EXPERT_LANE0 + gidx * EXPERTS_PER_GROUP
    emask = (lane >= first) & (lane < first + EXPERTS_PER_GROUP)
    el = jnp.where(emask, lg, neg)
    v1 = jnp.max(el, axis=-1, keepdims=True)
    i1 = first_lane_of_max(el, v1)
    el2 = jnp.where(lane == i1, neg, el)
    v2 = jnp.max(el2, axis=-1, keepdims=True)
    i2 = first_lane_of_max(el2, v2)
    e2 = jnp.exp(v2 - v1)
    w1 = 1.0 / (1.0 + e2)
    w2 = e2 / (1.0 + e2)
    comb = jnp.where(lane == i1, w1 * g_gate, 0.0) + jnp.where(lane == i2, w2 * g_gate, 0.0)
    return jnp.where(lane == 0.0, gidx, comb)


def _mixer_kernel(x_ref, h0_ref, c4_ref, c31_ref, vec_ref, cw4_ref, cw31_ref,
                  w_in_ref, w_gate_ref, w_rnn_ref, w_conv_ref, w_out_ref, wr_hi_ref, wr_lo_ref, br_ref,
                  x1c_ref, h_out_ref, c4_out_ref, c31_out_ref,
                  xn_b, ux_s, uy_s, cv_s, cg_s, grnn_s, gconv_s, glu_s, xc_s, xc_b, ga_s, gi_s, hy_s, cn_b, h_s,
                  x_s, st_s, st2_s, *, t_tile, b_blk, batch_major):
    i = pl.program_id(1)
    rows = t_tile * b_blk
    p4 = (RNN_CONV_W - 1) * b_blk
    p31 = (CONF_CONV_W - 1) * b_blk

    @pl.when(i == 0)
    def _():
        h_s[...] = h0_ref[...]
        ux_s[0:p4, :] = c4_ref[...]
        glu_s[0:p31, :] = c31_ref[...]

    def pk(r0):
        return pl.ds(r0, PACKED_ROWS)

    def vec(row):
        return vec_ref[row:row + 1, :]

    def x_rows(r0):
        if not batch_major:
            return x_ref[pk(r0), :]
        t = r0 // b_blk
        return jnp.concatenate([x_ref[:, t + k, :] for k in range(PACKED_ROWS // b_blk)], axis=0)

    def norm1_stats(r0):
        x = x_rows(r0)
        x_s[pk(r0), :] = x
        ms = jnp.mean(x * x, axis=-1, keepdims=True)
        st_s[pk(r0), :] = _on_lanes(lax.rsqrt(ms + EPS))
    _row_loop(rows, PACKED_ROWS, norm1_stats, unroll=STATS_UNROLL)

    def norm1_apply(r0):
        xn_b[pk(r0), :] = (x_s[pk(r0), :] * _across(st_s[pk(r0), :]) * vec(V_NORM1)).astype(BF16)
    _row_loop(rows, PACKED_ROWS, norm1_apply, unroll=4)

    def in_proj(group):
        return _dot(xn_b[...], w_in_ref[:, group * D:(group + 1) * D])
    ux_s[p4:p4 + rows, :] = in_proj(0)
    uy_s[...] = in_proj(1)
    cv_s[...] = in_proj(2)
    cg_s[...] = in_proj(3)
    grnn_s[...] = in_proj(4)
    gconv_s[...] = in_proj(5)

    def conv4_glu(r0):
        acc = ux_s[pk(r0), :] * cw4_ref[0:1, :]
        for k in range(1, RNN_CONV_W):
            acc = acc + ux_s[pk(r0 + k * b_blk), :] * cw4_ref[k:k + 1, :]
        xc = acc + vec(V_CONV4_B)
        xc_s[pk(r0), :] = xc
        xc_b[pk(r0), :] = xc.astype(BF16)
        glu_s[pk(p31 + r0), :] = cv_s[pk(r0), :] * _sigmoid(cg_s[pk(r0), :])
    _row_loop(rows, PACKED_ROWS, conv4_glu, unroll=2)

    strip_t = min(SUBLANES, t_tile)
    n_strip_t = t_tile // strip_t
    assert n_strip_t * strip_t == t_tile

    def conv31_strip(q, carry):
        jb = q // n_strip_t
        tc = q - jb * n_strip_t
        r0 = pl.multiple_of(tc * (strip_t * b_blk) + jb * SUBLANES, SUBLANES)
        for col in range(N_COLS):
            cs = slice(col * LANES, (col + 1) * LANES)
            xs = [glu_s[pl.ds(r0 + j * b_blk, SUBLANES), cs] for j in range(strip_t + CONF_CONV_W - 1)]
            accs = [None] * strip_t
            for k in range(CONF_CONV_W):
                w = cw31_ref[k:k + 1, cs]
                for o in range(strip_t):
                    term = xs[o + k] * w
                    accs[o] = term if k == 0 else accs[o] + term
            bias = vec_ref[V_CONV31_B:V_CONV31_B + 1, cs]
            for o in range(strip_t):
                cv_s[pl.ds(r0 + o * b_blk, SUBLANES), cs] = accs[o] + bias
        return carry
    lax.fori_loop(0, (b_blk // SUBLANES) * n_strip_t, conv31_strip, 0)

    for blk in range(N_GATE_BLOCKS):
        cols = slice(blk * MXU_DIM, (blk + 1) * MXU_DIM)
        g = _dot(xc_b[:, cols], w_gate_ref[blk])
        ga_s[:, cols] = g[:, :MXU_DIM]
        gi_s[:, cols] = g[:, MXU_DIM:]

    c_lam = -LRU_C * jax.nn.softplus(-vec(V_LAMBDA))

    def lru_coeffs(r0):
        r = _sigmoid(ga_s[pk(r0), :] + vec(V_BA))
        ig = _sigmoid(gi_s[pk(r0), :] + vec(V_BI))
        log_a = c_lam * r
        a = jnp.exp(log_a)
        mult = jnp.sqrt(-jnp.tanh(log_a) * (a * a + 1.0))
        ga_s[pk(r0), :] = a
        gi_s[pk(r0), :] = mult * (ig * xc_s[pk(r0), :])
        uy_s[pk(r0), :] = jax.nn.gelu(uy_s[pk(r0), :])
    _row_loop(rows, PACKED_ROWS, lru_coeffs, unroll=2)

    def scan_batch_slab(jb, carry):
        b0 = pl.multiple_of(jb * SUBLANES, SUBLANES)

        def step(t, h):
            sl = pl.ds(pl.multiple_of(t * b_blk + b0, SUBLANES), SUBLANES)
            h = ga_s[sl, :] * h + gi_s[sl, :]
            hy_s[sl, :] = h * uy_s[sl, :]
            return h

        h_s[pl.ds(b0, SUBLANES), :] = lax.fori_loop(0, t_tile, step, h_s[pl.ds(b0, SUBLANES), :],
                                                    unroll=min(8, t_tile))
        return carry
    lax.fori_loop(0, b_blk // SUBLANES, scan_batch_slab, 0)

    def ln_stats(r0):
        cc = cv_s[pk(r0), :]
        mu = jnp.mean(cc, axis=-1, keepdims=True)
        cen = cc - mu
        var = jnp.mean(cen * cen, axis=-1, keepdims=True)
        st_s[pk(r0), :] = _on_lanes(mu)
        st2_s[pk(r0), :] = _on_lanes(lax.rsqrt(var + EPS))
    _row_loop(rows, PACKED_ROWS, ln_stats, unroll=STATS_UNROLL)

    def ln_silu(r0):
        cen = cv_s[pk(r0), :] - _across(st_s[pk(r0), :])
        y = cen * _across(st2_s[pk(r0), :]) * vec(V_LN_G) + vec(V_LN_B)
        cn_b[pk(r0), :] = (y * _sigmoid(y)).astype(BF16)
    _row_loop(rows, PACKED_ROWS, ln_silu, unroll=4)

    ga_s[...] = _dot(hy_s[...].astype(BF16), w_rnn_ref[...])
    gi_s[...] = _dot(cn_b[...], w_conv_ref[...])

    def merge(r0):
        xn_b[pk(r0), :] = (_sigmoid(grnn_s[pk(r0), :]) * ga_s[pk(r0), :]
                           + _sigmoid(gconv_s[pk(r0), :]) * gi_s[pk(r0), :]).astype(BF16)
    _row_loop(rows, PACKED_ROWS, merge, unroll=2)
    cg_s[...] = _dot(xn_b[...], w_out_ref[...])

    def residual_stats(r0):
        x1 = x_s[pk(r0), :] + cg_s[pk(r0), :]
        x1c_ref[pk(r0), 0:D] = x1
        ms = jnp.mean(x1 * x1, axis=-1, keepdims=True)
        st_s[pk(r0), :] = _on_lanes(lax.rsqrt(ms + EPS))
    _row_loop(rows, PACKED_ROWS, residual_stats, unroll=STATS_UNROLL)

    def norm2_split(r0):
        xn2 = x1c_ref[pk(r0), 0:D] * _across(st_s[pk(r0), :]) * vec(V_NORM2)
        hi = xn2.astype(BF16)
        cn_b[pk(r0), :] = hi
        xc_b[pk(r0), :] = (xn2 - hi.astype(F32)).astype(BF16)
    _row_loop(rows, PACKED_ROWS, norm2_split, unroll=4)

    logits = (_dot(cn_b[...], wr_hi_ref[...]) + _dot(xc_b[...], wr_hi_ref[...])
              + _dot(cn_b[...], wr_lo_ref[...]) + br_ref[...])
    x1c_ref[:, D:D + LANES] = _route_info(logits)

    for k in range(RNN_CONV_W - 1):
        ux_s[k * b_blk:(k + 1) * b_blk, :] = ux_s[(t_tile + k) * b_blk:(t_tile + k + 1) * b_blk, :]
    for k in range(CONF_CONV_W - 1):
        glu_s[k * b_blk:(k + 1) * b_blk, :] = glu_s[(t_tile + k) * b_blk:(t_tile + k + 1) * b_blk, :]

    @pl.when(i == pl.num_programs(1) - 1)
    def _():
        h_out_ref[...] = h_s[...]
        c4_out_ref[...] = ux_s[0:p4, :]
        c31_out_ref[...] = glu_s[0:p31, :]


def _mixer_call(x, h0, c4, c31, wts, *, t_tile, b_blk, batch_major=False):
    if batch_major:
        assert x.shape[0] == b_blk and PACKED_ROWS % b_blk == 0
        n_b, total_rows = 1, x.shape[0] * x.shape[1]
    else:
        n_b, total_rows, _ = x.shape
    rows = t_tile * b_blk
    n_t = total_rows // rows
    assert n_t * rows == total_rows and b_blk % SUBLANES == 0 and rows % PACKED_ROWS == 0
    assert t_tile >= RNN_CONV_W - 1
    p4 = (RNN_CONV_W - 1) * b_blk
    p31 = (CONF_CONV_W - 1) * b_blk

    def tile_spec(width):
        return pl.BlockSpec((None, rows, width), lambda j, i: (j, i, 0))

    def state_spec(n_rows):
        return pl.BlockSpec((None, n_rows, D), lambda j, i: (j, 0, 0))

    def const_spec(arr):
        nd = arr.ndim
        return pl.BlockSpec(arr.shape, lambda j, i: (0,) * nd, pipeline_mode=pl.Buffered(1))

    plane = pltpu.VMEM((rows, D), F32)
    plane_b = pltpu.VMEM((rows, D), BF16)
    return pl.pallas_call(
        functools.partial(_mixer_kernel, t_tile=t_tile, b_blk=b_blk, batch_major=batch_major),
        grid=(n_b, n_t),
        in_specs=[pl.BlockSpec((b_blk, t_tile, D), lambda j, i: (0, i, 0)) if batch_major else tile_spec(D),
                  state_spec(b_blk), state_spec(p4), state_spec(p31)]
                 + [const_spec(w) for w in wts],
        out_specs=[pl.BlockSpec((rows, TOKEN_W), lambda j, i: (j * n_t + i, 0)),
                   state_spec(b_blk), state_spec(p4), state_spec(p31)],
        out_shape=[jax.ShapeDtypeStruct((n_b * total_rows, TOKEN_W), F32),
                   jax.ShapeDtypeStruct((n_b, b_blk, D), F32),
                   jax.ShapeDtypeStruct((n_b, p4, D), F32),
                   jax.ShapeDtypeStruct((n_b, p31, D), F32)],
        scratch_shapes=[plane_b, pltpu.VMEM((p4 + rows, D), F32), plane, plane, plane, plane, plane,
                        pltpu.VMEM((p31 + rows, D), F32), plane, plane_b, plane, plane, plane, plane_b,
                        pltpu.VMEM((b_blk, D), F32), plane, pltpu.VMEM((rows, LANES), F32),
                        pltpu.VMEM((rows, LANES), F32)],
        compiler_params=pltpu.CompilerParams(dimension_semantics=("arbitrary", "arbitrary"),
                                             vmem_limit_bytes=VMEM_LIMIT),
        name="mixer",
    )(x, h0, c4, c31, *wts)


def _row_copies(n, make_copy):
    def issue(r, carry):
        make_copy(r).start()
        return carry
    lax.fori_loop(0, n, issue, 0, unroll=8)


def _dispatch_kernel(slot_ref, pad0_ref, padn_ref, tok_a_ref, tok_b_ref, sorted_hbm, zero_s, sem, pad_sem,
                     *, n_a_tiles):
    i = pl.program_id(0)

    def scatter(tok_ref):
        def row_copy(r):
            return pltpu.make_async_copy(tok_ref.at[pl.ds(r, 1), :],
                                         sorted_hbm.at[pl.ds(slot_ref[i * COPY_TILE + r], 1), :], sem)
        _row_copies(COPY_TILE, row_copy)
        pltpu.make_async_copy(tok_ref, sorted_hbm.at[pl.ds(0, COPY_TILE), :], sem).wait()

    @pl.when(i < n_a_tiles)
    def _():
        scatter(tok_a_ref)

    @pl.when(i >= n_a_tiles)
    def _():
        scatter(tok_b_ref)

    @pl.when(i == pl.num_programs(0) - 1)
    def _():
        zero_s[...] = jnp.zeros_like(zero_s)
        for region in range(N_PAD_REGIONS):
            def pad_copy(k, region=region):
                return pltpu.make_async_copy(zero_s.at[pl.ds(0, 1), :],
                                             sorted_hbm.at[pl.ds(pad0_ref[region] + k, 1), :], pad_sem)

            def issue(k, carry):
                pad_copy(k).start()
                return carry

            def drain(k, carry):
                pad_copy(k).wait()
                return carry
            lax.fori_loop(0, padn_ref[region], issue, 0)
            lax.fori_loop(0, padn_ref[region], drain, 0)


def _dispatch_call(tokens_a, tokens_b, slot, pad0, padn, *, n_sorted):
    n_a, n_b = tokens_a.shape[0], tokens_b.shape[0]
    assert n_a % COPY_TILE == 0 and n_b % COPY_TILE == 0
    n_a_tiles = n_a // COPY_TILE
    return pl.pallas_call(
        functools.partial(_dispatch_kernel, n_a_tiles=n_a_tiles),
        grid_spec=pltpu.PrefetchScalarGridSpec(
            num_scalar_prefetch=3, grid=((n_a + n_b) // COPY_TILE,),
            in_specs=[pl.BlockSpec((COPY_TILE, TOKEN_W), lambda i, *_: (jnp.minimum(i, n_a_tiles - 1), 0)),
                      pl.BlockSpec((COPY_TILE, TOKEN_W), lambda i, *_: (jnp.maximum(i - n_a_tiles, 0), 0))],
            out_specs=pl.BlockSpec(memory_space=pl.ANY),
            scratch_shapes=[pltpu.VMEM((SUBLANES, TOKEN_W), F32), pltpu.SemaphoreType.DMA(()),
                            pltpu.SemaphoreType.DMA(())]),
        out_shape=jax.ShapeDtypeStruct((n_sorted, TOKEN_W), F32),
        compiler_params=pltpu.CompilerParams(dimension_semantics=("arbitrary",)),
        name="moe_dispatch",
    )(slot, pad0, padn, tokens_a, tokens_b)


def _moe_kernel(tile_group_ref, n_tiles_ref, tok_ref, wg_ref, wu_ref, wd_ref, gains_ref, y_ref,
                xn_b, he_b, y_s, st_s):
    i = pl.program_id(0)
    live = i < n_tiles_ref[0]

    def pk(r0):
        return pl.ds(r0, PACKED_ROWS)

    @pl.when(live)
    def _():
        def norm2_stats(r0):
            x1 = tok_ref[pk(r0), 0:D]
            ms = jnp.mean(x1 * x1, axis=-1, keepdims=True)
            st_s[pk(r0), :] = _on_lanes(lax.rsqrt(ms + EPS))
        _row_loop(MOE_TILE, PACKED_ROWS, norm2_stats, unroll=STATS_UNROLL)

        def norm2(r0):
            xn_b[pk(r0), :] = (tok_ref[pk(r0), 0:D] * _across(st_s[pk(r0), :]) * gains_ref[0:1, :]).astype(BF16)
        _row_loop(MOE_TILE, PACKED_ROWS, norm2, unroll=4)

        info = tok_ref[:, D:D + LANES]
        lane = lax.broadcasted_iota(jnp.int32, info.shape, 1)
        lane0 = EXPERT_LANE0 + tile_group_ref[i] * EXPERTS_PER_GROUP
        for e in range(EXPERTS_PER_GROUP):
            scale = jnp.sum(jnp.where(lane == lane0 + e, info, 0.0), axis=-1, keepdims=True)
            gate = _dot(xn_b[...], wg_ref[e])
            he = gate * _sigmoid(gate) * _dot(xn_b[...], wu_ref[e])
            he_b[:, e * D_EXPERT:(e + 1) * D_EXPERT] = (he * scale).astype(BF16)
        y_s[...] = _dot(he_b[...], wd_ref[...])

        def residual_stats(r0):
            y = tok_ref[pk(r0), 0:D] + y_s[pk(r0), :]
            y_s[pk(r0), :] = y
            ms = jnp.mean(y * y, axis=-1, keepdims=True)
            st_s[pk(r0), :] = _on_lanes(lax.rsqrt(ms + EPS))
        _row_loop(MOE_TILE, PACKED_ROWS, residual_stats, unroll=STATS_UNROLL)

        def final_norm(r0):
            y_ref[pk(r0), :] = y_s[pk(r0), :] * _across(st_s[pk(r0), :]) * gains_ref[1:2, :]
        _row_loop(MOE_TILE, PACKED_ROWS, final_norm, unroll=4)

    @pl.when(jnp.logical_not(live))
    def _():
        y_ref[...] = jnp.zeros_like(y_ref)


def _moe_call(sorted_tokens, tile_group, n_tiles, wg, wu, wd, gains):
    n_sorted = sorted_tokens.shape[0]
    assert n_sorted % MOE_TILE == 0
    hidden = EXPERTS_PER_GROUP * D_EXPERT

    def tile(i, tile_group_ref, n_tiles_ref):
        return (jnp.minimum(i, n_tiles_ref[0] - 1), 0)

    def group(i, tile_group_ref, n_tiles_ref):
        return (tile_group_ref[jnp.minimum(i, n_tiles_ref[0] - 1)], 0, 0)

    return pl.pallas_call(
        _moe_kernel,
        grid_spec=pltpu.PrefetchScalarGridSpec(
            num_scalar_prefetch=2, grid=(n_sorted // MOE_TILE,),
            in_specs=[pl.BlockSpec((MOE_TILE, TOKEN_W), tile),
                      pl.BlockSpec((EXPERTS_PER_GROUP, D, D_EXPERT), group),
                      pl.BlockSpec((EXPERTS_PER_GROUP, D, D_EXPERT), group),
                      pl.BlockSpec((None, hidden, D), group),
                      pl.BlockSpec((SUBLANES, D), lambda i, *_: (0, 0))],
            out_specs=pl.BlockSpec((MOE_TILE, D), lambda i, *_: (i, 0)),
            scratch_shapes=[pltpu.VMEM((MOE_TILE, D), BF16), pltpu.VMEM((MOE_TILE, hidden), BF16),
                            pltpu.VMEM((MOE_TILE, D), F32), pltpu.VMEM((MOE_TILE, LANES), F32)]),
        out_shape=jax.ShapeDtypeStruct((n_sorted, D), F32),
        compiler_params=pltpu.CompilerParams(dimension_semantics=("arbitrary",),
                                             vmem_limit_bytes=VMEM_LIMIT),
        name="moe_grouped",
    )(tile_group, n_tiles, sorted_tokens, wg, wu, wd, gains)


def _collect_kernel(src_ref, y_hbm, o_ref, sem):
    i = pl.program_id(0)

    def row_copy(r):
        return pltpu.make_async_copy(y_hbm.at[pl.ds(src_ref[i * COPY_TILE + r], 1), :],
                                     o_ref.at[pl.ds(r, 1), :], sem)
    _row_copies(COPY_TILE, row_copy)
    pltpu.make_async_copy(y_hbm.at[pl.ds(0, COPY_TILE), :], o_ref, sem).wait()


def _collect_call(y_sorted, src):
    n = src.shape[0]
    assert n % COPY_TILE == 0
    return pl.pallas_call(
        _collect_kernel,
        grid_spec=pltpu.PrefetchScalarGridSpec(
            num_scalar_prefetch=1, grid=(n // COPY_TILE,),
            in_specs=[pl.BlockSpec(memory_space=pl.ANY)],
            out_specs=pl.BlockSpec((COPY_TILE, D), lambda i, *_: (i, 0)),
            scratch_shapes=[pltpu.SemaphoreType.DMA(())]),
        out_shape=jax.ShapeDtypeStruct((n, D), F32),
        compiler_params=pltpu.CompilerParams(dimension_semantics=("arbitrary",)),
        name="moe_collect",
    )(src, y_sorted)


def _routing_tables(group, n_tile_cap):
    onehot = (group[:, None] == jnp.arange(N_GROUPS, dtype=jnp.int32)[None, :]).astype(jnp.int32)
    running = jnp.cumsum(onehot, axis=0)
    rank = jnp.sum(running * onehot, axis=1) - 1
    counts = running[-1]
    tiles_per_group = (counts + MOE_TILE - 1) // MOE_TILE
    end_tile = jnp.cumsum(tiles_per_group)
    first_row = (end_tile - tiles_per_group) * MOE_TILE
    slot = jnp.sum(first_row[None, :] * onehot, axis=1) + rank
    tile_ids = jnp.arange(n_tile_cap, dtype=jnp.int32)
    tile_group = jnp.minimum(jnp.sum((tile_ids[:, None] >= end_tile[None, :]).astype(jnp.int32), axis=1),
                             N_GROUPS - 1)
    used_rows = end_tile[-1:] * MOE_TILE
    pad0 = jnp.concatenate([first_row + counts, used_rows])
    padn = jnp.concatenate([tiles_per_group * MOE_TILE - counts, n_tile_cap * MOE_TILE - used_rows])
    return (slot.astype(jnp.int32), tile_group.astype(jnp.int32), end_tile[-1:].astype(jnp.int32),
            pad0.astype(jnp.int32), padn.astype(jnp.int32))


def _block_diag_gates(w_a, w_i):
    def bd(w):
        w = w.reshape(N_GATE_BLOCKS, HEADS_PER_BLOCK, HEAD_DIM, HEAD_DIM)
        eye = jnp.eye(HEADS_PER_BLOCK, dtype=w.dtype)
        full = jnp.einsum("bhij,hg->bhigj", w, eye)
        return full.reshape(N_GATE_BLOCKS, MXU_DIM, MXU_DIM)
    return jnp.concatenate([bd(w_a), bd(w_i)], axis=-1).astype(BF16)


def _to_time_major(a, n_b):
    b, t, d = a.shape
    return a.reshape(n_b, b // n_b, t, d).transpose(0, 2, 1, 3).reshape(n_b, t * (b // n_b), d)


def _from_time_major(a, t):
    n_b, rows, d = a.shape
    bb = rows // t
    return a.reshape(n_b, t, bb, d).transpose(0, 2, 1, 3).reshape(n_b * bb, t, d)


def kernel(x_prompt, x_sample, state_rglru_h, state_rglru_conv, state_conformer_conv, meta_tokens, norm1_g, w_in, rnn_conv_w, rnn_conv_b, w_rg_a, b_rg_a, w_rg_i, b_rg_i, rg_lambda, w_rnn_proj, conf_conv_w, conf_conv_b, conf_ln_g, conf_ln_b, w_conv_proj, w_out, norm2_g, w_group, b_group, w_erouter, b_erouter, w_exp_gate, w_exp_up, w_exp_down, final_norm_g):
    assert w_in.shape[0] == 1, "single-layer trunk"
    bsz, seq, _ = x_prompt.shape
    dec_b, dec_t, _ = x_sample.shape

    vec_rows = [norm1_g[0], rnn_conv_b[0], b_rg_a[0], b_rg_i[0], rg_lambda[0], conf_conv_b[0], conf_ln_g[0],
                conf_ln_b[0], norm2_g[0]]
    vec = jnp.zeros((N_VEC_ROWS, D), F32).at[:len(vec_rows)].set(jnp.stack(vec_rows).astype(F32))
    cw4 = jnp.zeros((SUBLANES, D), F32).at[:RNN_CONV_W].set(rnn_conv_w[0])
    cw31 = jnp.zeros((4 * SUBLANES, D), F32).at[:CONF_CONV_W].set(conf_conv_w[0])
    w_router = jnp.zeros((D, LANES), F32)
    w_router = w_router.at[:, :N_GROUPS].set(w_group[0]).at[:, N_GROUPS:N_GROUPS + N_EXPERTS].set(w_erouter[0])
    wr_hi = w_router.astype(BF16)
    wr_lo = (w_router - wr_hi.astype(F32)).astype(BF16)
    b_router = jnp.zeros((1, LANES), F32)
    b_router = b_router.at[0, :N_GROUPS].set(b_group[0]).at[0, N_GROUPS:N_GROUPS + N_EXPERTS].set(
        b_erouter[0].reshape(-1))
    wts = (vec, cw4, cw31, w_in[0].astype(BF16), _block_diag_gates(w_rg_a[0], w_rg_i[0]),
           w_rnn_proj[0].astype(BF16), w_conv_proj[0].astype(BF16), w_out[0].astype(BF16), wr_hi, wr_lo, b_router)
    wg = w_exp_gate[0].astype(BF16)
    wu = w_exp_up[0].astype(BF16)
    wd = w_exp_down[0].astype(BF16).reshape(N_GROUPS, EXPERTS_PER_GROUP * D_EXPERT, D)
    gains = jnp.zeros((SUBLANES, D), F32).at[0].set(norm2_g[0]).at[1].set(final_norm_g)

    n_p = bsz * seq
    n_s = dec_b * dec_t

    meta_tm = jnp.broadcast_to(meta_tokens[:, None, :], (N_META, bsz, D)).reshape(1, N_META * bsz, D)
    _, h_m, c4_m, c31_m = _mixer_call(
        meta_tm, jnp.zeros((1, bsz, D), F32), jnp.zeros((1, (RNN_CONV_W - 1) * bsz, D), F32),
        jnp.zeros((1, (CONF_CONV_W - 1) * bsz, D), F32), wts, t_tile=N_META, b_blk=bsz)

    tok_p, h_p, c4_p, c31_p = _mixer_call(
        x_prompt, h_m, c4_m, c31_m, wts, t_tile=32, b_blk=bsz, batch_major=True)

    n_b = 4
    bb = dec_b // n_b
    tok_s, h_s, c4_s, c31_s = _mixer_call(
        _to_time_major(x_sample, n_b), state_rglru_h[0].reshape(n_b, bb, D),
        _to_time_major(state_rglru_conv[0], n_b), _to_time_major(state_conformer_conv[0], n_b), wts,
        t_tile=dec_t, b_blk=bb)

    n_tile_cap = (n_p + n_s) // MOE_TILE + N_GROUPS
    group = jnp.concatenate([tok_p[:, D], tok_s[:, D]]).astype(jnp.int32)
    slot, tile_group, n_tiles, pad0, padn = _routing_tables(group, n_tile_cap)
    sorted_tokens = _dispatch_call(tok_p, tok_s, slot, pad0, padn, n_sorted=n_tile_cap * MOE_TILE)
    y_sorted = _moe_call(sorted_tokens, tile_group, n_tiles, wg, wu, wd, gains)
    src_p = slot[:n_p].reshape(seq, bsz).T.reshape(-1)
    src_s = slot[n_p:].reshape(n_b, dec_t, bb).transpose(0, 2, 1).reshape(-1)
    y_prompt = _collect_call(y_sorted, src_p).reshape(bsz, seq, D)
    y_sample = _collect_call(y_sorted, src_s).reshape(dec_b, dec_t, D)

    return (y_prompt, y_sample,
            h_p.reshape(1, bsz, D), _from_time_major(c4_p, RNN_CONV_W - 1)[None],
            _from_time_major(c31_p, CONF_CONV_W - 1)[None],
            h_s.reshape(1, dec_b, D), _from_time_major(c4_s, RNN_CONV_W - 1)[None],
            _from_time_major(c31_s, CONF_CONV_W - 1)[None])
```

```python
import functools

import jax
import jax.numpy as jnp
from jax import lax
from jax.experimental import pallas as pl
from jax.experimental.pallas import tpu as pltpu

D = 1024
N_META = 16
N_HEADS = 16
HEAD_DIM = D // N_HEADS
RNN_CONV_W = 4
CONF_CONV_W = 31
LRU_C = 8.0
N_GROUPS = 4
EXPERTS_PER_GROUP = 8
N_EXPERTS = N_GROUPS * EXPERTS_PER_GROUP
D_EXPERT = D // 4
EPS = 1e-6

SUBLANES = 8
PACKED_ROWS = 16
LANES = 128
N_COLS = D // LANES
MXU_DIM = 256
HEADS_PER_BLOCK = MXU_DIM // HEAD_DIM
N_GATE_BLOCKS = D // MXU_DIM
VMEM_LIMIT = 56 * 1024 * 1024
TOKEN_W = D + LANES
EXPERT_LANE0 = N_GROUPS
MOE_TILE = 256
COPY_TILE = 2048
STATS_UNROLL = 16
N_PAD_REGIONS = N_GROUPS + 1

F32 = jnp.float32
BF16 = jnp.bfloat16

(V_NORM1, V_CONV4_B, V_BA, V_BI, V_LAMBDA, V_CONV31_B, V_LN_G, V_LN_B, V_NORM2) = range(9)
N_VEC_ROWS = 16


def _dot(a, b):
    return jnp.dot(a, b, preferred_element_type=F32)


def _sigmoid(x):
    return 0.5 * jnp.tanh(0.5 * x) + 0.5


def _on_lanes(col):
    return jnp.broadcast_to(col, (col.shape[0], LANES))


def _across(tile):
    return jnp.concatenate([tile] * N_COLS, axis=1)


def _row_loop(n_rows, chunk, body, unroll):
    n = n_rows // chunk
    assert n * chunk == n_rows

    def wrapped(s, carry):
        body(pl.multiple_of(s * chunk, chunk))
        return carry
    lax.fori_loop(0, n, wrapped, 0, unroll=min(unroll, n))


def _route_info(lg):
    neg = float("-inf")
    lane = lax.broadcasted_iota(jnp.int32, lg.shape, 1).astype(F32)
    no_lane = float(LANES)

    def first_lane_of_max(vals, vmax):
        return jnp.min(jnp.where(vals == vmax, lane, no_lane), axis=-1, keepdims=True)

    gmask = lane < N_GROUPS
    gl = jnp.where(gmask, lg, neg)
    gmax = jnp.max(gl, axis=-1, keepdims=True)
    gidx = first_lane_of_max(gl, gmax)
    gsum = jnp.sum(jnp.where(gmask, jnp.exp(gl - gmax), 0.0), axis=-1, keepdims=True)
    g_gate = 1.0 / gsum
    first = EXPERT_LANE0 + gidx * EXPERTS_PER_GROUP
    emask = (lane >= first) & (lane < first + EXPERTS_PER_GROUP)
    el = jnp.where(emask, lg, neg)
    v1 = jnp.max(el, axis=-1, keepdims=True)
    i1 = first_lane_of_max(el, v1)
    el2 = jnp.where(lane == i1, neg, el)
    v2 = jnp.max(el2, axis=-1, keepdims=True)
    i2 = first_lane_of_max(el2, v2)
    e2 = jnp.exp(v2 - v1)
    w1 = 1.0 / (1.0 + e2)
    w2 = e2 / (1.0 + e2)
    comb = jnp.where(lane == i1, w1 * g_gate, 0.0) + jnp.where(lane == i2, w2 * g_gate, 0.0)
    return jnp.where(lane == 0.0, gidx, comb)


def _mixer_kernel(x_ref, h0_ref, c4_ref, c31_ref, vec_ref, cw4_ref, cw31_ref,
                  w_in_ref, w_gate_ref, w_rnn_ref, w_conv_ref, w_out_ref, wr_hi_ref, wr_lo_ref, br_ref,
                  x1c_ref, h_out_ref, c4_out_ref, c31_out_ref,
                  xn_b, ux_s, uy_s, cv_s, cg_s, grnn_s, gconv_s, glu_s, xc_s, xc_b, ga_s, gi_s, hy_s, cn_b, h_s,
                  x_s, st_s, st2_s, *, t_tile, b_blk, batch_major):
    i = pl.program_id(1)
    rows = t_tile * b_blk
    p4 = (RNN_CONV_W - 1) * b_blk
    p31 = (CONF_CONV_W - 1) * b_blk

    @pl.when(i == 0)
    def _():
        h_s[...] = h0_ref[...]
        ux_s[0:p4, :] = c4_ref[...]
        glu_s[0:p31, :] = c31_ref[...]

    def pk(r0):
        return pl.ds(r0, PACKED_ROWS)

    def vec(row):
        return vec_ref[row:row + 1, :]

    def x_rows(r0):
        if not batch_major:
            return x_ref[pk(r0), :]
        t = r0 // b_blk
        return jnp.concatenate([x_ref[:, t + k, :] for k in range(PACKED_ROWS // b_blk)], axis=0)

    def norm1_stats(r0):
        x = x_rows(r0)
        x_s[pk(r0), :] = x
        ms = jnp.mean(x * x, axis=-1, keepdims=True)
        st_s[pk(r0), :] = _on_lanes(lax.rsqrt(ms + EPS))
    _row_loop(rows, PACKED_ROWS, norm1_stats, unroll=STATS_UNROLL)

    def norm1_apply(r0):
        xn_b[pk(r0), :] = (x_s[pk(r0), :] * _across(st_s[pk(r0), :]) * vec(V_NORM1)).astype(BF16)
    _row_loop(rows, PACKED_ROWS, norm1_apply, unroll=4)

    def in_proj(group):
        return _dot(xn_b[...], w_in_ref[:, group * D:(group + 1) * D])
    ux_s[p4:p4 + rows, :] = in_proj(0)
    uy_s[...] = in_proj(1)
    cv_s[...] = in_proj(2)
    cg_s[...] = in_proj(3)
    grnn_s[...] = in_proj(4)
    gconv_s[...] = in_proj(5)

    def conv4_glu(r0):
        acc = ux_s[pk(r0), :] * cw4_ref[0:1, :]
        for k in range(1, RNN_CONV_W):
            acc = acc + ux_s[pk(r0 + k * b_blk), :] * cw4_ref[k:k + 1, :]
        xc = acc + vec(V_CONV4_B)
        xc_s[pk(r0), :] = xc
        xc_b[pk(r0), :] = xc.astype(BF16)
        glu_s[pk(p31 + r0), :] = cv_s[pk(r0), :] * _sigmoid(cg_s[pk(r0), :])
    _row_loop(rows, PACKED_ROWS, conv4_glu, unroll=2)

    strip_t = min(SUBLANES, t_tile)
    n_strip_t = t_tile // strip_t
    assert n_strip_t * strip_t == t_tile

    def conv31_strip(q, carry):
        jb = q // n_strip_t
        tc = q - jb * n_strip_t
        r0 = pl.multiple_of(tc * (strip_t * b_blk) + jb * SUBLANES, SUBLANES)
        for col in range(N_COLS):
            cs = slice(col * LANES, (col + 1) * LANES)
            xs = [glu_s[pl.ds(r0 + j * b_blk, SUBLANES), cs] for j in range(strip_t + CONF_CONV_W - 1)]
            accs = [None] * strip_t
            for k in range(CONF_CONV_W):
                w = cw31_ref[k:k + 1, cs]
                for o in range(strip_t):
                    term = xs[o + k] * w
                    accs[o] = term if k == 0 else accs[o] + term
            bias = vec_ref[V_CONV31_B:V_CONV31_B + 1, cs]
            for o in range(strip_t):
                cv_s[pl.ds(r0 + o * b_blk, SUBLANES), cs] = accs[o] + bias
        return carry
    lax.fori_loop(0, (b_blk // SUBLANES) * n_strip_t, conv31_strip, 0)

    for blk in range(N_GATE_BLOCKS):
        cols = slice(blk * MXU_DIM, (blk + 1) * MXU_DIM)
        g = _dot(xc_b[:, cols], w_gate_ref[blk])
        ga_s[:, cols] = g[:, :MXU_DIM]
        gi_s[:, cols] = g[:, MXU_DIM:]

    c_lam = -LRU_C * jax.nn.softplus(-vec(V_LAMBDA))

    def lru_coeffs(r0):
        r = _sigmoid(ga_s[pk(r0), :] + vec(V_BA))
        ig = _sigmoid(gi_s[pk(r0), :] + vec(V_BI))
        log_a = c_lam * r
        a = jnp.exp(log_a)
        mult = jnp.sqrt(-jnp.tanh(log_a) * (a * a + 1.0))
        ga_s[pk(r0), :] = a
        gi_s[pk(r0), :] = mult * (ig * xc_s[pk(r0), :])
        uy_s[pk(r0), :] = jax.nn.gelu(uy_s[pk(r0), :])
    _row_loop(rows, PACKED_ROWS, lru_coeffs, unroll=2)

    def scan_batch_slab(jb, carry):
        b0 = pl.multiple_of(jb * SUBLANES, SUBLANES)

        def step(t, h):
            sl = pl.ds(pl.multiple_of(t * b_blk + b0, SUBLANES), SUBLANES)
            h = ga_s[sl, :] * h + gi_s[sl, :]
            hy_s[sl, :] = h * uy_s[sl, :]
            return h

        h_s[pl.ds(b0, SUBLANES), :] = lax.fori_loop(0, t_tile, step, h_s[pl.ds(b0, SUBLANES), :],
                                                    unroll=min(8, t_tile))
        return carry
    lax.fori_loop(0, b_blk // SUBLANES, scan_batch_slab, 0)

    def ln_stats(r0):
        cc = cv_s[pk(r0), :]
        mu = jnp.mean(cc, axis=-1, keepdims=True)
        cen = cc - mu
        var = jnp.mean(cen * cen, axis=-1, keepdims=True)
        st_s[pk(r0), :] = _on_lanes(mu)
        st2_s[pk(r0), :] = _on_lanes(lax.rsqrt(var + EPS))
    _row_loop(rows, PACKED_ROWS, ln_stats, unroll=STATS_UNROLL)

    def ln_silu(r0):
        cen = cv_s[pk(r0), :] - _across(st_s[pk(r0), :])
        y = cen * _across(st2_s[pk(r0), :]) * vec(V_LN_G) + vec(V_LN_B)
        cn_b[pk(r0), :] = (y * _sigmoid(y)).astype(BF16)
    _row_loop(rows, PACKED_ROWS, ln_silu, unroll=4)

    ga_s[...] = _dot(hy_s[...].astype(BF16), w_rnn_ref[...])
    gi_s[...] = _dot(cn_b[...], w_conv_ref[...])

    def merge(r0):
        xn_b[pk(r0), :] = (_sigmoid(grnn_s[pk(r0), :]) * ga_s[pk(r0), :]
                           + _sigmoid(gconv_s[pk(r0), :]) * gi_s[pk(r0), :]).astype(BF16)
    _row_loop(rows, PACKED_ROWS, merge, unroll=2)
    cg_s[...] = _dot(xn_b[...], w_out_ref[...])

    def residual_stats(r0):
        x1 = x_s[pk(r0), :] + cg_s[pk(r0), :]
        x1c_ref[pk(r0), 0:D] = x1
        ms = jnp.mean(x1 * x1, axis=-1, keepdims=True)
        st_s[pk(r0), :] = _on_lanes(lax.rsqrt(ms + EPS))
    _row_loop(rows, PACKED_ROWS, residual_stats, unroll=STATS_UNROLL)

    def norm2_split(r0):
        xn2 = x1c_ref[pk(r0), 0:D] * _across(st_s[pk(r0), :]) * vec(V_NORM2)
        hi = xn2.astype(BF16)
        cn_b[pk(r0), :] = hi
        xc_b[pk(r0), :] = (xn2 - hi.astype(F32)).astype(BF16)
    _row_loop(rows, PACKED_ROWS, norm2_split, unroll=4)

    logits = (_dot(cn_b[...], wr_hi_ref[...]) + _dot(xc_b[...], wr_hi_ref[...])
              + _dot(cn_b[...], wr_lo_ref[...]) + br_ref[...])
    x1c_ref[:, D:D + LANES] = _route_info(logits)

    for k in range(RNN_CONV_W - 1):
        ux_s[k * b_blk:(k + 1) * b_blk, :] = ux_s[(t_tile + k) * b_blk:(t_tile + k + 1) * b_blk, :]
    for k in range(CONF_CONV_W - 1):
        glu_s[k * b_blk:(k + 1) * b_blk, :] = glu_s[(t_tile + k) * b_blk:(t_tile + k + 1) * b_blk, :]

    @pl.when(i == pl.num_programs(1) - 1)
    def _():
        h_out_ref[...] = h_s[...]
        c4_out_ref[...] = ux_s[0:p4, :]
        c31_out_ref[...] = glu_s[0:p31, :]


def _mixer_call(x, h0, c4, c31, wts, *, t_tile, b_blk, batch_major=False):
    if batch_major:
        assert x.shape[0] == b_blk and PACKED_ROWS % b_blk == 0
        n_b, total_rows = 1, x.shape[0] * x.shape[1]
    else:
        n_b, total_rows, _ = x.shape
    rows = t_tile * b_blk
    n_t = total_rows // rows
    assert n_t * rows == total_rows and b_blk % SUBLANES == 0 and rows % PACKED_ROWS == 0
    assert t_tile >= RNN_CONV_W - 1
    p4 = (RNN_CONV_W - 1) * b_blk
    p31 = (CONF_CONV_W - 1) * b_blk

    def tile_spec(width):
        return pl.BlockSpec((None, rows, width), lambda j, i: (j, i, 0))

    def state_spec(n_rows):
        return pl.BlockSpec((None, n_rows, D), lambda j, i: (j, 0, 0))

    def const_spec(arr):
        nd = arr.ndim
        return pl.BlockSpec(arr.shape, lambda j, i: (0,) * nd, pipeline_mode=pl.Buffered(1))

    plane = pltpu.VMEM((rows, D), F32)
    plane_b = pltpu.VMEM((rows, D), BF16)
    return pl.pallas_call(
        functools.partial(_mixer_kernel, t_tile=t_tile, b_blk=b_blk, batch_major=batch_major),
        grid=(n_b, n_t),
        in_specs=[pl.BlockSpec((b_blk, t_tile, D), lambda j, i: (0, i, 0)) if batch_major else tile_spec(D),
                  state_spec(b_blk), state_spec(p4), state_spec(p31)]
                 + [const_spec(w) for w in wts],
        out_specs=[pl.BlockSpec((rows, TOKEN_W), lambda j, i: (j * n_t + i, 0)),
                   state_spec(b_blk), state_spec(p4), state_spec(p31)],
        out_shape=[jax.ShapeDtypeStruct((n_b * total_rows, TOKEN_W), F32),
                   jax.ShapeDtypeStruct((n_b, b_blk, D), F32),
                   jax.ShapeDtypeStruct((n_b, p4, D), F32),
                   jax.ShapeDtypeStruct((n_b, p31, D), F32)],
        scratch_shapes=[plane_b, pltpu.VMEM((p4 + rows, D), F32), plane, plane, plane, plane, plane,
                        pltpu.VMEM((p31 + rows, D), F32), plane, plane_b, plane, plane, plane, plane_b,
                        pltpu.VMEM((b_blk, D), F32), plane, pltpu.VMEM((rows, LANES), F32),
                        pltpu.VMEM((rows, LANES), F32)],
        compiler_params=pltpu.CompilerParams(dimension_semantics=("arbitrary", "arbitrary"),
                                             vmem_limit_bytes=VMEM_LIMIT),
        name="mixer",
    )(x, h0, c4, c31, *wts)


def _row_copies(n, make_copy):
    def issue(r, carry):
        make_copy(r).start()
        return carry
    lax.fori_loop(0, n, issue, 0, unroll=8)


def _copy_tile(n_rows):
    tile = COPY_TILE
    while n_rows % tile:
        tile //= 2
    return tile


def _dispatch_kernel(slot_ref, pad0_ref, padn_ref, tok_a_ref, tok_b_ref, sorted_hbm, zero_s, sem, pad_sem,
                     *, n_a_tiles):
    i = pl.program_id(0)
    tile_a, tile_b = tok_a_ref.shape[0], tok_b_ref.shape[0]

    def scatter(tok_ref, token0):
        n = tok_ref.shape[0]

        def row_copy(r):
            return pltpu.make_async_copy(tok_ref.at[pl.ds(r, 1), :],
                                         sorted_hbm.at[pl.ds(slot_ref[token0 + r], 1), :], sem)
        _row_copies(n, row_copy)
        pltpu.make_async_copy(tok_ref, sorted_hbm.at[pl.ds(0, n), :], sem).wait()

    @pl.when(i < n_a_tiles)
    def _():
        scatter(tok_a_ref, i * tile_a)

    @pl.when(i >= n_a_tiles)
    def _():
        scatter(tok_b_ref, n_a_tiles * tile_a + (i - n_a_tiles) * tile_b)

    @pl.when(i == pl.num_programs(0) - 1)
    def _():
        zero_s[...] = jnp.zeros_like(zero_s)
        for region in range(N_PAD_REGIONS):
            def pad_copy(k, region=region):
                return pltpu.make_async_copy(zero_s.at[pl.ds(0, 1), :],
                                             sorted_hbm.at[pl.ds(pad0_ref[region] + k, 1), :], pad_sem)

            def issue(k, carry):
                pad_copy(k).start()
                return carry

            def drain(k, carry):
                pad_copy(k).wait()
                return carry
            lax.fori_loop(0, padn_ref[region], issue, 0)
            lax.fori_loop(0, padn_ref[region], drain, 0)


def _dispatch_call(tokens_a, tokens_b, slot, pad0, padn, *, n_sorted):
    n_a, n_b = tokens_a.shape[0], tokens_b.shape[0]
    tile_a, tile_b = _copy_tile(n_a), _copy_tile(n_b)
    n_a_tiles = n_a // tile_a
    return pl.pallas_call(
        functools.partial(_dispatch_kernel, n_a_tiles=n_a_tiles),
        grid_spec=pltpu.PrefetchScalarGridSpec(
            num_scalar_prefetch=3, grid=(n_a_tiles + n_b // tile_b,),
            in_specs=[pl.BlockSpec((tile_a, TOKEN_W), lambda i, *_: (jnp.minimum(i, n_a_tiles - 1), 0)),
                      pl.BlockSpec((tile_b, TOKEN_W), lambda i, *_: (jnp.maximum(i - n_a_tiles, 0), 0))],
            out_specs=pl.BlockSpec(memory_space=pl.ANY),
            scratch_shapes=[pltpu.VMEM((SUBLANES, TOKEN_W), F32), pltpu.SemaphoreType.DMA(()),
                            pltpu.SemaphoreType.DMA(())]),
        out_shape=jax.ShapeDtypeStruct((n_sorted, TOKEN_W), F32),
        compiler_params=pltpu.CompilerParams(dimension_semantics=("arbitrary",), vmem_limit_bytes=VMEM_LIMIT),
        name="moe_dispatch",
    )(slot, pad0, padn, tokens_a, tokens_b)


def _moe_kernel(tile_group_ref, n_tiles_ref, tok_ref, wg_ref, wu_ref, wd_ref, gains_ref, y_ref,
                xn_b, he_b, y_s, st_s):
    i = pl.program_id(0)
    live = i < n_tiles_ref[0]

    def pk(r0):
        return pl.ds(r0, PACKED_ROWS)

    @pl.when(live)
    def _():
        def norm2_stats(r0):
            x1 = tok_ref[pk(r0), 0:D]
            ms = jnp.mean(x1 * x1, axis=-1, keepdims=True)
            st_s[pk(r0), :] = _on_lanes(lax.rsqrt(ms + EPS))
        _row_loop(MOE_TILE, PACKED_ROWS, norm2_stats, unroll=STATS_UNROLL)

        def norm2(r0):
            xn_b[pk(r0), :] = (tok_ref[pk(r0), 0:D] * _across(st_s[pk(r0), :]) * gains_ref[0:1, :]).astype(BF16)
        _row_loop(MOE_TILE, PACKED_ROWS, norm2, unroll=4)

        info = tok_ref[:, D:D + LANES]
        lane = lax.broadcasted_iota(jnp.int32, info.shape, 1)
        lane0 = EXPERT_LANE0 + tile_group_ref[i] * EXPERTS_PER_GROUP
        for e in range(EXPERTS_PER_GROUP):
            scale = jnp.sum(jnp.where(lane == lane0 + e, info, 0.0), axis=-1, keepdims=True)
            gate = _dot(xn_b[...], wg_ref[e])
            he = gate * _sigmoid(gate) * _dot(xn_b[...], wu_ref[e])
            he_b[:, e * D_EXPERT:(e + 1) * D_EXPERT] = (he * scale).astype(BF16)
        y_s[...] = _dot(he_b[...], wd_ref[...])

        def residual_stats(r0):
            y = tok_ref[pk(r0), 0:D] + y_s[pk(r0), :]
            y_s[pk(r0), :] = y
            ms = jnp.mean(y * y, axis=-1, keepdims=True)
            st_s[pk(r0), :] = _on_lanes(lax.rsqrt(ms + EPS))
        _row_loop(MOE_TILE, PACKED_ROWS, residual_stats, unroll=STATS_UNROLL)

        def final_norm(r0):
            y_ref[pk(r0), :] = y_s[pk(r0), :] * _across(st_s[pk(r0), :]) * gains_ref[1:2, :]
        _row_loop(MOE_TILE, PACKED_ROWS, final_norm, unroll=4)

    @pl.when(jnp.logical_not(live))
    def _():
        y_ref[...] = jnp.zeros_like(y_ref)


def _moe_call(sorted_tokens, tile_group, n_tiles, wg, wu, wd, gains):
    n_sorted = sorted_tokens.shape[0]
    assert n_sorted % MOE_TILE == 0
    hidden = EXPERTS_PER_GROUP * D_EXPERT

    def tile(i, tile_group_ref, n_tiles_ref):
        return (jnp.minimum(i, n_tiles_ref[0] - 1), 0)

    def group(i, tile_group_ref, n_tiles_ref):
        return (tile_group_ref[jnp.minimum(i, n_tiles_ref[0] - 1)], 0, 0)

    return pl.pallas_call(
        _moe_kernel,
        grid_spec=pltpu.PrefetchScalarGridSpec(
            num_scalar_prefetch=2, grid=(n_sorted // MOE_TILE,),
            in_specs=[pl.BlockSpec((MOE_TILE, TOKEN_W), tile),
                      pl.BlockSpec((EXPERTS_PER_GROUP, D, D_EXPERT), group),
                      pl.BlockSpec((EXPERTS_PER_GROUP, D, D_EXPERT), group),
                      pl.BlockSpec((None, hidden, D), group),
                      pl.BlockSpec((SUBLANES, D), lambda i, *_: (0, 0))],
            out_specs=pl.BlockSpec((MOE_TILE, D), lambda i, *_: (i, 0)),
            scratch_shapes=[pltpu.VMEM((MOE_TILE, D), BF16), pltpu.VMEM((MOE_TILE, hidden), BF16),
                            pltpu.VMEM((MOE_TILE, D), F32), pltpu.VMEM((MOE_TILE, LANES), F32)]),
        out_shape=jax.ShapeDtypeStruct((n_sorted, D), F32),
        compiler_params=pltpu.CompilerParams(dimension_semantics=("arbitrary",),
                                             vmem_limit_bytes=VMEM_LIMIT),
        name="moe_grouped",
    )(tile_group, n_tiles, sorted_tokens, wg, wu, wd, gains)


def _collect_kernel(src_ref, y_hbm, o_ref, sem):
    i = pl.program_id(0)
    tile = o_ref.shape[0]

    def row_copy(r):
        return pltpu.make_async_copy(y_hbm.at[pl.ds(src_ref[i * tile + r], 1), :],
                                     o_ref.at[pl.ds(r, 1), :], sem)
    _row_copies(tile, row_copy)
    pltpu.make_async_copy(y_hbm.at[pl.ds(0, tile), :], o_ref, sem).wait()


def _collect_call(y_sorted, src):
    n = src.shape[0]
    tile = _copy_tile(n)
    return pl.pallas_call(
        _collect_kernel,
        grid_spec=pltpu.PrefetchScalarGridSpec(
            num_scalar_prefetch=1, grid=(n // tile,),
            in_specs=[pl.BlockSpec(memory_space=pl.ANY)],
            out_specs=pl.BlockSpec((tile, D), lambda i, *_: (i, 0)),
            scratch_shapes=[pltpu.SemaphoreType.DMA(())]),
        out_shape=jax.ShapeDtypeStruct((n, D), F32),
        compiler_params=pltpu.CompilerParams(dimension_semantics=("arbitrary",), vmem_limit_bytes=VMEM_LIMIT),
        name="moe_collect",
    )(src, y_sorted)


def _routing_tables(group, n_tile_cap):
    onehot = (group[:, None] == jnp.arange(N_GROUPS, dtype=jnp.int32)[None, :]).astype(jnp.int32)
    running = jnp.cumsum(onehot, axis=0)
    rank = jnp.sum(running * onehot, axis=1) - 1
    counts = running[-1]
    tiles_per_group = (counts + MOE_TILE - 1) // MOE_TILE
    end_tile = jnp.cumsum(tiles_per_group)
    first_row = (end_tile - tiles_per_group) * MOE_TILE
    slot = jnp.sum(first_row[None, :] * onehot, axis=1) + rank
    tile_ids = jnp.arange(n_tile_cap, dtype=jnp.int32)
    tile_group = jnp.minimum(jnp.sum((tile_ids[:, None] >= end_tile[None, :]).astype(jnp.int32), axis=1),
                             N_GROUPS - 1)
    used_rows = end_tile[-1:] * MOE_TILE
    pad0 = jnp.concatenate([first_row + counts, used_rows])
    padn = jnp.concatenate([tiles_per_group * MOE_TILE - counts, n_tile_cap * MOE_TILE - used_rows])
    return (slot.astype(jnp.int32), tile_group.astype(jnp.int32), end_tile[-1:].astype(jnp.int32),
            pad0.astype(jnp.int32), padn.astype(jnp.int32))


def _block_diag_gates(w_a, w_i):
    def bd(w):
        w = w.reshape(N_GATE_BLOCKS, HEADS_PER_BLOCK, HEAD_DIM, HEAD_DIM)
        eye = jnp.eye(HEADS_PER_BLOCK, dtype=w.dtype)
        full = jnp.einsum("bhij,hg->bhigj", w, eye)
        return full.reshape(N_GATE_BLOCKS, MXU_DIM, MXU_DIM)
    return jnp.concatenate([bd(w_a), bd(w_i)], axis=-1).astype(BF16)


def _to_time_major(a, n_b):
    b, t, d = a.shape
    return a.reshape(n_b, b // n_b, t, d).transpose(0, 2, 1, 3).reshape(n_b, t * (b // n_b), d)


def _from_time_major(a, t):
    n_b, rows, d = a.shape
    bb = rows // t
    return a.reshape(n_b, t, bb, d).transpose(0, 2, 1, 3).reshape(n_b * bb, t, d)


def kernel(x_prompt, x_sample, state_rglru_h, state_rglru_conv, state_conformer_conv, meta_tokens, norm1_g, w_in, rnn_conv_w, rnn_conv_b, w_rg_a, b_rg_a, w_rg_i, b_rg_i, rg_lambda, w_rnn_proj, conf_conv_w, conf_conv_b, conf_ln_g, conf_ln_b, w_conv_proj, w_out, norm2_g, w_group, b_group, w_erouter, b_erouter, w_exp_gate, w_exp_up, w_exp_down, final_norm_g):
    assert w_in.shape[0] == 1, "single-layer trunk"
    bsz, seq, _ = x_prompt.shape
    dec_b, dec_t, _ = x_sample.shape

    vec_rows = [norm1_g[0], rnn_conv_b[0], b_rg_a[0], b_rg_i[0], rg_lambda[0], conf_conv_b[0], conf_ln_g[0],
                conf_ln_b[0], norm2_g[0]]
    vec = jnp.zeros((N_VEC_ROWS, D), F32).at[:len(vec_rows)].set(jnp.stack(vec_rows).astype(F32))
    cw4 = jnp.zeros((SUBLANES, D), F32).at[:RNN_CONV_W].set(rnn_conv_w[0])
    cw31 = jnp.zeros((4 * SUBLANES, D), F32).at[:CONF_CONV_W].set(conf_conv_w[0])
    w_router = jnp.zeros((D, LANES), F32)
    w_router = w_router.at[:, :N_GROUPS].set(w_group[0]).at[:, N_GROUPS:N_GROUPS + N_EXPERTS].set(w_erouter[0])
    wr_hi = w_router.astype(BF16)
    wr_lo = (w_router - wr_hi.astype(F32)).astype(BF16)
    b_router = jnp.zeros((1, LANES), F32)
    b_router = b_router.at[0, :N_GROUPS].set(b_group[0]).at[0, N_GROUPS:N_GROUPS + N_EXPERTS].set(
        b_erouter[0].reshape(-1))
    wts = (vec, cw4, cw31, w_in[0].astype(BF16), _block_diag_gates(w_rg_a[0], w_rg_i[0]),
           w_rnn_proj[0].astype(BF16), w_conv_proj[0].astype(BF16), w_out[0].astype(BF16), wr_hi, wr_lo, b_router)
    wg = w_exp_gate[0].astype(BF16)
    wu = w_exp_up[0].astype(BF16)
    wd = w_exp_down[0].astype(BF16).reshape(N_GROUPS, EXPERTS_PER_GROUP * D_EXPERT, D)
    gains = jnp.zeros((SUBLANES, D), F32).at[0].set(norm2_g[0]).at[1].set(final_norm_g)

    n_p = bsz * seq
    n_s = dec_b * dec_t

    meta_tm = jnp.broadcast_to(meta_tokens[:, None, :], (N_META, bsz, D)).reshape(1, N_META * bsz, D)
    _, h_m, c4_m, c31_m = _mixer_call(
        meta_tm, jnp.zeros((1, bsz, D), F32), jnp.zeros((1, (RNN_CONV_W - 1) * bsz, D), F32),
        jnp.zeros((1, (CONF_CONV_W - 1) * bsz, D), F32), wts, t_tile=N_META, b_blk=bsz)

    tok_p, h_p, c4_p, c31_p = _mixer_call(
        x_prompt, h_m, c4_m, c31_m, wts, t_tile=32, b_blk=bsz, batch_major=True)

    n_b = 4
    bb = dec_b // n_b
    tok_s, h_s, c4_s, c31_s = _mixer_call(
        _to_time_major(x_sample, n_b), state_rglru_h[0].reshape(n_b, bb, D),
        _to_time_major(state_rglru_conv[0], n_b), _to_time_major(state_conformer_conv[0], n_b), wts,
        t_tile=dec_t, b_blk=bb)

    n_tile_cap = (n_p + n_s) // MOE_TILE + N_GROUPS
    group = jnp.concatenate([tok_p[:, D], tok_s[:, D]]).astype(jnp.int32)
    slot, tile_group, n_tiles, pad0, padn = _routing_tables(group, n_tile_cap)
    sorted_tokens = _dispatch_call(tok_p, tok_s, slot, pad0, padn, n_sorted=n_tile_cap * MOE_TILE)
    y_sorted = _moe_call(sorted_tokens, tile_group, n_tiles, wg, wu, wd, gains)
    src_p = slot[:n_p].reshape(seq, bsz).T.reshape(-1)
    src_s = slot[n_p:].reshape(n_b, dec_t, bb).transpose(0, 2, 1).reshape(-1)
    y_prompt = _collect_call(y_sorted, src_p).reshape(bsz, seq, D)
    y_sample = _collect_call(y_sorted, src_s).reshape(dec_b, dec_t, D)

    return (y_prompt, y_sample,
            h_p.reshape(1, bsz, D), _from_time_major(c4_p, RNN_CONV_W - 1)[None],
            _from_time_major(c31_p, CONF_CONV_W - 1)[None],
            h_s.reshape(1, dec_b, D), _from_time_major(c4_s, RNN_CONV_W - 1)[None],
            _from_time_major(c31_s, CONF_CONV_W - 1)[None])
```

```python
import functools

import jax
import jax.numpy as jnp
from jax import lax
from jax.experimental import pallas as pl
from jax.experimental.pallas import tpu as pltpu

D = 1024
N_META = 16
N_HEADS = 16
HEAD_DIM = D // N_HEADS
RNN_CONV_W = 4
CONF_CONV_W = 31
LRU_C = 8.0
N_GROUPS = 4
EXPERTS_PER_GROUP = 8
N_EXPERTS = N_GROUPS * EXPERTS_PER_GROUP
D_EXPERT = D // 4
EPS = 1e-6

SUBLANES = 8
PACKED_ROWS = 16
LANES = 128
N_COLS = D // LANES
MXU_DIM = 256
HEADS_PER_BLOCK = MXU_DIM // HEAD_DIM
N_GATE_BLOCKS = D // MXU_DIM
VMEM_LIMIT = 56 * 1024 * 1024
TOKEN_W = D + LANES
EXPERT_LANE0 = N_GROUPS
MOE_TILE = 256
COPY_TILE = 2048
ROWS_PER_ISSUE = 8
STATS_UNROLL = 16
N_PAD_REGIONS = N_GROUPS + 1

F32 = jnp.float32
BF16 = jnp.bfloat16

(V_NORM1, V_CONV4_B, V_BA, V_BI, V_LAMBDA, V_CONV31_B, V_LN_G, V_LN_B, V_NORM2) = range(9)
N_VEC_ROWS = 16


def _dot(a, b):
    return jnp.dot(a, b, preferred_element_type=F32)


def _sigmoid(x):
    return 0.5 * jnp.tanh(0.5 * x) + 0.5


def _on_lanes(col):
    return jnp.broadcast_to(col, (col.shape[0], LANES))


def _across(tile):
    return jnp.concatenate([tile] * N_COLS, axis=1)


def _row_loop(n_rows, chunk, body, unroll):
    n = n_rows // chunk
    assert n * chunk == n_rows

    def wrapped(s, carry):
        body(pl.multiple_of(s * chunk, chunk))
        return carry
    lax.fori_loop(0, n, wrapped, 0, unroll=min(unroll, n))


def _route_info(lg):
    neg = float("-inf")
    lane = lax.broadcasted_iota(jnp.int32, lg.shape, 1).astype(F32)
    no_lane = float(LANES)

    def first_lane_of_max(vals, vmax):
        return jnp.min(jnp.where(vals == vmax, lane, no_lane), axis=-1, keepdims=True)

    gmask = lane < N_GROUPS
    gl = jnp.where(gmask, lg, neg)
    gmax = jnp.max(gl, axis=-1, keepdims=True)
    gidx = first_lane_of_max(gl, gmax)
    gsum = jnp.sum(jnp.where(gmask, jnp.exp(gl - gmax), 0.0), axis=-1, keepdims=True)
    g_gate = 1.0 / gsum
    first = EXPERT_LANE0 + gidx * EXPERTS_PER_GROUP
    emask = (lane >= first) & (lane < first + EXPERTS_PER_GROUP)
    el = jnp.where(emask, lg, neg)
    v1 = jnp.max(el, axis=-1, keepdims=True)
    i1 = first_lane_of_max(el, v1)
    el2 = jnp.where(lane == i1, neg, el)
    v2 = jnp.max(el2, axis=-1, keepdims=True)
    i2 = first_lane_of_max(el2, v2)
    e2 = jnp.exp(v2 - v1)
    w1 = 1.0 / (1.0 + e2)
    w2 = e2 / (1.0 + e2)
    comb = jnp.where(lane == i1, w1 * g_gate, 0.0) + jnp.where(lane == i2, w2 * g_gate, 0.0)
    return jnp.where(lane == 0.0, gidx, comb)


def _mixer_kernel(x_ref, h0_ref, c4_ref, c31_ref, vec_ref, cw4_ref, cw31_ref,
                  w_in_ref, w_gate_ref, w_rnn_ref, w_conv_ref, w_out_ref, wr_hi_ref, wr_lo_ref, br_ref,
                  x1c_ref, h_out_ref, c4_out_ref, c31_out_ref,
                  xn_b, ux_s, uy_s, cv_s, cg_s, grnn_s, gconv_s, glu_s, xc_s, xc_b, ga_s, gi_s, hy_s, cn_b, h_s,
                  x_s, st_s, st2_s, *, t_tile, b_blk, batch_major):
    i = pl.program_id(1)
    rows = t_tile * b_blk
    p4 = (RNN_CONV_W - 1) * b_blk
    p31 = (CONF_CONV_W - 1) * b_blk

    @pl.when(i == 0)
    def _():
        h_s[...] = h0_ref[...]
        ux_s[0:p4, :] = c4_ref[...]
        glu_s[0:p31, :] = c31_ref[...]

    def pk(r0):
        return pl.ds(r0, PACKED_ROWS)

    def vec(row):
        return vec_ref[row:row + 1, :]

    def x_rows(r0):
        if not batch_major:
            return x_ref[pk(r0), :]
        t = r0 // b_blk
        return jnp.concatenate([x_ref[:, t + k, :] for k in range(PACKED_ROWS // b_blk)], axis=0)

    def norm1_stats(r0):
        x = x_rows(r0)
        x_s[pk(r0), :] = x
        ms = jnp.mean(x * x, axis=-1, keepdims=True)
        st_s[pk(r0), :] = _on_lanes(lax.rsqrt(ms + EPS))
    _row_loop(rows, PACKED_ROWS, norm1_stats, unroll=STATS_UNROLL)

    def norm1_apply(r0):
        xn_b[pk(r0), :] = (x_s[pk(r0), :] * _across(st_s[pk(r0), :]) * vec(V_NORM1)).astype(BF16)
    _row_loop(rows, PACKED_ROWS, norm1_apply, unroll=4)

    def in_proj(group):
        return _dot(xn_b[...], w_in_ref[:, group * D:(group + 1) * D])
    ux_s[p4:p4 + rows, :] = in_proj(0)
    uy_s[...] = in_proj(1)
    cv_s[...] = in_proj(2)
    cg_s[...] = in_proj(3)
    grnn_s[...] = in_proj(4)
    gconv_s[...] = in_proj(5)

    def conv4_glu(r0):
        acc = ux_s[pk(r0), :] * cw4_ref[0:1, :]
        for k in range(1, RNN_CONV_W):
            acc = acc + ux_s[pk(r0 + k * b_blk), :] * cw4_ref[k:k + 1, :]
        xc = acc + vec(V_CONV4_B)
        xc_s[pk(r0), :] = xc
        xc_b[pk(r0), :] = xc.astype(BF16)
        glu_s[pk(p31 + r0), :] = cv_s[pk(r0), :] * _sigmoid(cg_s[pk(r0), :])
    _row_loop(rows, PACKED_ROWS, conv4_glu, unroll=2)

    strip_t = min(SUBLANES, t_tile)
    n_strip_t = t_tile // strip_t
    assert n_strip_t * strip_t == t_tile

    def conv31_strip(q, carry):
        jb = q // n_strip_t
        tc = q - jb * n_strip_t
        r0 = pl.multiple_of(tc * (strip_t * b_blk) + jb * SUBLANES, SUBLANES)
        for col in range(N_COLS):
            cs = slice(col * LANES, (col + 1) * LANES)
            xs = [glu_s[pl.ds(r0 + j * b_blk, SUBLANES), cs] for j in range(strip_t + CONF_CONV_W - 1)]
            accs = [None] * strip_t
            for k in range(CONF_CONV_W):
                w = cw31_ref[k:k + 1, cs]
                for o in range(strip_t):
                    term = xs[o + k] * w
                    accs[o] = term if k == 0 else accs[o] + term
            bias = vec_ref[V_CONV31_B:V_CONV31_B + 1, cs]
            for o in range(strip_t):
                cv_s[pl.ds(r0 + o * b_blk, SUBLANES), cs] = accs[o] + bias
        return carry
    lax.fori_loop(0, (b_blk // SUBLANES) * n_strip_t, conv31_strip, 0)

    for blk in range(N_GATE_BLOCKS):
        cols = slice(blk * MXU_DIM, (blk + 1) * MXU_DIM)
        g = _dot(xc_b[:, cols], w_gate_ref[blk])
        ga_s[:, cols] = g[:, :MXU_DIM]
        gi_s[:, cols] = g[:, MXU_DIM:]

    c_lam = -LRU_C * jax.nn.softplus(-vec(V_LAMBDA))

    def lru_coeffs(r0):
        r = _sigmoid(ga_s[pk(r0), :] + vec(V_BA))
        ig = _sigmoid(gi_s[pk(r0), :] + vec(V_BI))
        log_a = c_lam * r
        a = jnp.exp(log_a)
        mult = jnp.sqrt(-jnp.tanh(log_a) * (a * a + 1.0))
        ga_s[pk(r0), :] = a
        gi_s[pk(r0), :] = mult * (ig * xc_s[pk(r0), :])
        uy_s[pk(r0), :] = jax.nn.gelu(uy_s[pk(r0), :])
    _row_loop(rows, PACKED_ROWS, lru_coeffs, unroll=2)

    def scan_batch_slab(jb, carry):
        b0 = pl.multiple_of(jb * SUBLANES, SUBLANES)

        def step(t, h):
            sl = pl.ds(pl.multiple_of(t * b_blk + b0, SUBLANES), SUBLANES)
            h = ga_s[sl, :] * h + gi_s[sl, :]
            hy_s[sl, :] = h * uy_s[sl, :]
            return h

        h_s[pl.ds(b0, SUBLANES), :] = lax.fori_loop(0, t_tile, step, h_s[pl.ds(b0, SUBLANES), :],
                                                    unroll=min(8, t_tile))
        return carry
    lax.fori_loop(0, b_blk // SUBLANES, scan_batch_slab, 0)

    def ln_stats(r0):
        cc = cv_s[pk(r0), :]
        mu = jnp.mean(cc, axis=-1, keepdims=True)
        cen = cc - mu
        var = jnp.mean(cen * cen, axis=-1, keepdims=True)
        st_s[pk(r0), :] = _on_lanes(mu)
        st2_s[pk(r0), :] = _on_lanes(lax.rsqrt(var + EPS))
    _row_loop(rows, PACKED_ROWS, ln_stats, unroll=STATS_UNROLL)

    def ln_silu(r0):
        cen = cv_s[pk(r0), :] - _across(st_s[pk(r0), :])
        y = cen * _across(st2_s[pk(r0), :]) * vec(V_LN_G) + vec(V_LN_B)
        cn_b[pk(r0), :] = (y * _sigmoid(y)).astype(BF16)
    _row_loop(rows, PACKED_ROWS, ln_silu, unroll=4)

    ga_s[...] = _dot(hy_s[...].astype(BF16), w_rnn_ref[...])
    gi_s[...] = _dot(cn_b[...], w_conv_ref[...])

    def merge(r0):
        xn_b[pk(r0), :] = (_sigmoid(grnn_s[pk(r0), :]) * ga_s[pk(r0), :]
                           + _sigmoid(gconv_s[pk(r0), :]) * gi_s[pk(r0), :]).astype(BF16)
    _row_loop(rows, PACKED_ROWS, merge, unroll=2)
    cg_s[...] = _dot(xn_b[...], w_out_ref[...])

    def residual_stats(r0):
        x1 = x_s[pk(r0), :] + cg_s[pk(r0), :]
        x1c_ref[pk(r0), 0:D] = x1
        ms = jnp.mean(x1 * x1, axis=-1, keepdims=True)
        st_s[pk(r0), :] = _on_lanes(lax.rsqrt(ms + EPS))
    _row_loop(rows, PACKED_ROWS, residual_stats, unroll=STATS_UNROLL)

    def norm2_split(r0):
        xn2 = x1c_ref[pk(r0), 0:D] * _across(st_s[pk(r0), :]) * vec(V_NORM2)
        hi = xn2.astype(BF16)
        cn_b[pk(r0), :] = hi
        xc_b[pk(r0), :] = (xn2 - hi.astype(F32)).astype(BF16)
    _row_loop(rows, PACKED_ROWS, norm2_split, unroll=4)

    logits = (_dot(cn_b[...], wr_hi_ref[...]) + _dot(xc_b[...], wr_hi_ref[...])
              + _dot(cn_b[...], wr_lo_ref[...]) + br_ref[...])
    x1c_ref[:, D:D + LANES] = _route_info(logits)

    for k in range(RNN_CONV_W - 1):
        ux_s[k * b_blk:(k + 1) * b_blk, :] = ux_s[(t_tile + k) * b_blk:(t_tile + k + 1) * b_blk, :]
    for k in range(CONF_CONV_W - 1):
        glu_s[k * b_blk:(k + 1) * b_blk, :] = glu_s[(t_tile + k) * b_blk:(t_tile + k + 1) * b_blk, :]

    @pl.when(i == pl.num_programs(1) - 1)
    def _():
        h_out_ref[...] = h_s[...]
        c4_out_ref[...] = ux_s[0:p4, :]
        c31_out_ref[...] = glu_s[0:p31, :]


def _mixer_call(x, h0, c4, c31, wts, *, t_tile, b_blk, batch_major=False):
    if batch_major:
        assert x.shape[0] == b_blk and PACKED_ROWS % b_blk == 0
        n_b, total_rows = 1, x.shape[0] * x.shape[1]
    else:
        n_b, total_rows, _ = x.shape
    rows = t_tile * b_blk
    n_t = total_rows // rows
    assert n_t * rows == total_rows and b_blk % SUBLANES == 0 and rows % PACKED_ROWS == 0
    assert t_tile >= RNN_CONV_W - 1
    p4 = (RNN_CONV_W - 1) * b_blk
    p31 = (CONF_CONV_W - 1) * b_blk

    def tile_spec(width):
        return pl.BlockSpec((None, rows, width), lambda j, i: (j, i, 0))

    def state_spec(n_rows):
        return pl.BlockSpec((None, n_rows, D), lambda j, i: (j, 0, 0))

    def const_spec(arr):
        nd = arr.ndim
        return pl.BlockSpec(arr.shape, lambda j, i: (0,) * nd, pipeline_mode=pl.Buffered(1))

    plane = pltpu.VMEM((rows, D), F32)
    plane_b = pltpu.VMEM((rows, D), BF16)
    return pl.pallas_call(
        functools.partial(_mixer_kernel, t_tile=t_tile, b_blk=b_blk, batch_major=batch_major),
        grid=(n_b, n_t),
        in_specs=[pl.BlockSpec((b_blk, t_tile, D), lambda j, i: (0, i, 0)) if batch_major else tile_spec(D),
                  state_spec(b_blk), state_spec(p4), state_spec(p31)]
                 + [const_spec(w) for w in wts],
        out_specs=[pl.BlockSpec((rows, TOKEN_W), lambda j, i: (j * n_t + i, 0)),
                   state_spec(b_blk), state_spec(p4), state_spec(p31)],
        out_shape=[jax.ShapeDtypeStruct((n_b * total_rows, TOKEN_W), F32),
                   jax.ShapeDtypeStruct((n_b, b_blk, D), F32),
                   jax.ShapeDtypeStruct((n_b, p4, D), F32),
                   jax.ShapeDtypeStruct((n_b, p31, D), F32)],
        scratch_shapes=[plane_b, pltpu.VMEM((p4 + rows, D), F32), plane, plane, plane, plane, plane,
                        pltpu.VMEM((p31 + rows, D), F32), plane, plane_b, plane, plane, plane, plane_b,
                        pltpu.VMEM((b_blk, D), F32), plane, pltpu.VMEM((rows, LANES), F32),
                        pltpu.VMEM((rows, LANES), F32)],
        compiler_params=pltpu.CompilerParams(dimension_semantics=("arbitrary", "arbitrary"),
                                             vmem_limit_bytes=VMEM_LIMIT),
        name="mixer",
    )(x, h0, c4, c31, *wts)


def _row_copies(n, make_copy):
    assert n % ROWS_PER_ISSUE == 0

    def issue(r0, carry):
        for k in range(ROWS_PER_ISSUE):
            make_copy(r0 * ROWS_PER_ISSUE + k).start(priority=k % 2)
        return carry
    lax.fori_loop(0, n // ROWS_PER_ISSUE, issue, 0)


def _copy_tile(n_rows):
    tile = COPY_TILE
    while n_rows % tile:
        tile //= 2
    return tile


def _dispatch_kernel(slot_ref, pad0_ref, padn_ref, tok_a_ref, tok_b_ref, sorted_hbm, zero_s, sem, pad_sem,
                     *, n_a_tiles):
    i = pl.program_id(0)
    tile_a, tile_b = tok_a_ref.shape[0], tok_b_ref.shape[0]

    def scatter(tok_ref, token0):
        n = tok_ref.shape[0]

        def row_copy(r):
            return pltpu.make_async_copy(tok_ref.at[pl.ds(r, 1), :],
                                         sorted_hbm.at[pl.ds(slot_ref[token0 + r], 1), :], sem)
        _row_copies(n, row_copy)
        pltpu.make_async_copy(tok_ref, sorted_hbm.at[pl.ds(0, n), :], sem).wait()

    @pl.when(i < n_a_tiles)
    def _():
        scatter(tok_a_ref, i * tile_a)

    @pl.when(i >= n_a_tiles)
    def _():
        scatter(tok_b_ref, n_a_tiles * tile_a + (i - n_a_tiles) * tile_b)

    @pl.when(i == pl.num_programs(0) - 1)
    def _():
        zero_s[...] = jnp.zeros_like(zero_s)
        for region in range(N_PAD_REGIONS):
            def pad_copy(k, region=region):
                return pltpu.make_async_copy(zero_s.at[pl.ds(0, 1), :],
                                             sorted_hbm.at[pl.ds(pad0_ref[region] + k, 1), :], pad_sem)

            def issue(k, carry):
                pad_copy(k).start()
                return carry

            def drain(k, carry):
                pad_copy(k).wait()
                return carry
            lax.fori_loop(0, padn_ref[region], issue, 0)
            lax.fori_loop(0, padn_ref[region], drain, 0)


def _dispatch_call(tokens_a, tokens_b, slot, pad0, padn, *, n_sorted):
    n_a, n_b = tokens_a.shape[0], tokens_b.shape[0]
    tile_a, tile_b = _copy_tile(n_a), _copy_tile(n_b)
    n_a_tiles = n_a // tile_a
    return pl.pallas_call(
        functools.partial(_dispatch_kernel, n_a_tiles=n_a_tiles),
        grid_spec=pltpu.PrefetchScalarGridSpec(
            num_scalar_prefetch=3, grid=(n_a_tiles + n_b // tile_b,),
            in_specs=[pl.BlockSpec((tile_a, TOKEN_W), lambda i, *_: (jnp.minimum(i, n_a_tiles - 1), 0)),
                      pl.BlockSpec((tile_b, TOKEN_W), lambda i, *_: (jnp.maximum(i - n_a_tiles, 0), 0))],
            out_specs=pl.BlockSpec(memory_space=pl.ANY),
            scratch_shapes=[pltpu.VMEM((SUBLANES, TOKEN_W), F32), pltpu.SemaphoreType.DMA(()),
                            pltpu.SemaphoreType.DMA(())]),
        out_shape=jax.ShapeDtypeStruct((n_sorted, TOKEN_W), F32),
        compiler_params=pltpu.CompilerParams(dimension_semantics=("arbitrary",), vmem_limit_bytes=VMEM_LIMIT),
        name="moe_dispatch",
    )(slot, pad0, padn, tokens_a, tokens_b)


def _moe_kernel(tile_group_ref, n_tiles_ref, tok_ref, wg_ref, wu_ref, wd_ref, gains_ref, y_ref,
                xn_b, he_b, y_s, st_s):
    i = pl.program_id(0)
    live = i < n_tiles_ref[0]

    def pk(r0):
        return pl.ds(r0, PACKED_ROWS)

    @pl.when(live)
    def _():
        def norm2_stats(r0):
            x1 = tok_ref[pk(r0), 0:D]
            ms = jnp.mean(x1 * x1, axis=-1, keepdims=True)
            st_s[pk(r0), :] = _on_lanes(lax.rsqrt(ms + EPS))
        _row_loop(MOE_TILE, PACKED_ROWS, norm2_stats, unroll=STATS_UNROLL)

        def norm2(r0):
            xn_b[pk(r0), :] = (tok_ref[pk(r0), 0:D] * _across(st_s[pk(r0), :]) * gains_ref[0:1, :]).astype(BF16)
        _row_loop(MOE_TILE, PACKED_ROWS, norm2, unroll=4)

        info = tok_ref[:, D:D + LANES]
        lane = lax.broadcasted_iota(jnp.int32, info.shape, 1)
        lane0 = EXPERT_LANE0 + tile_group_ref[i] * EXPERTS_PER_GROUP
        for e in range(EXPERTS_PER_GROUP):
            scale = jnp.sum(jnp.where(lane == lane0 + e, info, 0.0), axis=-1, keepdims=True)
            gate = _dot(xn_b[...], wg_ref[e])
            he = gate * _sigmoid(gate) * _dot(xn_b[...], wu_ref[e])
            he_b[:, e * D_EXPERT:(e + 1) * D_EXPERT] = (he * scale).astype(BF16)
        y_s[...] = _dot(he_b[...], wd_ref[...])

        def residual_stats(r0):
            y = tok_ref[pk(r0), 0:D] + y_s[pk(r0), :]
            y_s[pk(r0), :] = y
            ms = jnp.mean(y * y, axis=-1, keepdims=True)
            st_s[pk(r0), :] = _on_lanes(lax.rsqrt(ms + EPS))
        _row_loop(MOE_TILE, PACKED_ROWS, residual_stats, unroll=STATS_UNROLL)

        def final_norm(r0):
            y_ref[pk(r0), :] = y_s[pk(r0), :] * _across(st_s[pk(r0), :]) * gains_ref[1:2, :]
        _row_loop(MOE_TILE, PACKED_ROWS, final_norm, unroll=4)

    @pl.when(jnp.logical_not(live))
    def _():
        y_ref[...] = jnp.zeros_like(y_ref)


def _moe_call(sorted_tokens, tile_group, n_tiles, wg, wu, wd, gains):
    n_sorted = sorted_tokens.shape[0]
    assert n_sorted % MOE_TILE == 0
    hidden = EXPERTS_PER_GROUP * D_EXPERT

    def tile(i, tile_group_ref, n_tiles_ref):
        return (jnp.minimum(i, n_tiles_ref[0] - 1), 0)

    def group(i, tile_group_ref, n_tiles_ref):
        return (tile_group_ref[jnp.minimum(i, n_tiles_ref[0] - 1)], 0, 0)

    return pl.pallas_call(
        _moe_kernel,
        grid_spec=pltpu.PrefetchScalarGridSpec(
            num_scalar_prefetch=2, grid=(n_sorted // MOE_TILE,),
            in_specs=[pl.BlockSpec((MOE_TILE, TOKEN_W), tile),
                      pl.BlockSpec((EXPERTS_PER_GROUP, D, D_EXPERT), group),
                      pl.BlockSpec((EXPERTS_PER_GROUP, D, D_EXPERT), group),
                      pl.BlockSpec((None, hidden, D), group),
                      pl.BlockSpec((SUBLANES, D), lambda i, *_: (0, 0))],
            out_specs=pl.BlockSpec((MOE_TILE, D), lambda i, *_: (i, 0)),
            scratch_shapes=[pltpu.VMEM((MOE_TILE, D), BF16), pltpu.VMEM((MOE_TILE, hidden), BF16),
                            pltpu.VMEM((MOE_TILE, D), F32), pltpu.VMEM((MOE_TILE, LANES), F32)]),
        out_shape=jax.ShapeDtypeStruct((n_sorted, D), F32),
        compiler_params=pltpu.CompilerParams(dimension_semantics=("arbitrary",),
                                             vmem_limit_bytes=VMEM_LIMIT),
        name="moe_grouped",
    )(tile_group, n_tiles, sorted_tokens, wg, wu, wd, gains)


def _collect_kernel(src_ref, y_hbm, o_ref, sem):
    i = pl.program_id(0)
    tile = o_ref.shape[0]

    def row_copy(r):
        return pltpu.make_async_copy(y_hbm.at[pl.ds(src_ref[i * tile + r], 1), :],
                                     o_ref.at[pl.ds(r, 1), :], sem)
    _row_copies(tile, row_copy)
    pltpu.make_async_copy(y_hbm.at[pl.ds(0, tile), :], o_ref, sem).wait()


def _collect_call(y_sorted, src):
    n = src.shape[0]
    tile = _copy_tile(n)
    return pl.pallas_call(
        _collect_kernel,
        grid_spec=pltpu.PrefetchScalarGridSpec(
            num_scalar_prefetch=1, grid=(n // tile,),
            in_specs=[pl.BlockSpec(memory_space=pl.ANY)],
            out_specs=pl.BlockSpec((tile, D), lambda i, *_: (i, 0)),
            scratch_shapes=[pltpu.SemaphoreType.DMA(())]),
        out_shape=jax.ShapeDtypeStruct((n, D), F32),
        compiler_params=pltpu.CompilerParams(dimension_semantics=("arbitrary",), vmem_limit_bytes=VMEM_LIMIT),
        name="moe_collect",
    )(src, y_sorted)


def _routing_tables(group, n_tile_cap):
    onehot = (group[:, None] == jnp.arange(N_GROUPS, dtype=jnp.int32)[None, :]).astype(jnp.int32)
    running = jnp.cumsum(onehot, axis=0)
    rank = jnp.sum(running * onehot, axis=1) - 1
    counts = running[-1]
    tiles_per_group = (counts + MOE_TILE - 1) // MOE_TILE
    end_tile = jnp.cumsum(tiles_per_group)
    first_row = (end_tile - tiles_per_group) * MOE_TILE
    slot = jnp.sum(first_row[None, :] * onehot, axis=1) + rank
    tile_ids = jnp.arange(n_tile_cap, dtype=jnp.int32)
    tile_group = jnp.minimum(jnp.sum((tile_ids[:, None] >= end_tile[None, :]).astype(jnp.int32), axis=1),
                             N_GROUPS - 1)
    used_rows = end_tile[-1:] * MOE_TILE
    pad0 = jnp.concatenate([first_row + counts, used_rows])
    padn = jnp.concatenate([tiles_per_group * MOE_TILE - counts, n_tile_cap * MOE_TILE - used_rows])
    return (slot.astype(jnp.int32), tile_group.astype(jnp.int32), end_tile[-1:].astype(jnp.int32),
            pad0.astype(jnp.int32), padn.astype(jnp.int32))


def _block_diag_gates(w_a, w_i):
    def bd(w):
        w = w.reshape(N_GATE_BLOCKS, HEADS_PER_BLOCK, HEAD_DIM, HEAD_DIM)
        eye = jnp.eye(HEADS_PER_BLOCK, dtype=w.dtype)
        full = jnp.einsum("bhij,hg->bhigj", w, eye)
        return full.reshape(N_GATE_BLOCKS, MXU_DIM, MXU_DIM)
    return jnp.concatenate([bd(w_a), bd(w_i)], axis=-1).astype(BF16)


def _to_time_major(a, n_b):
    b, t, d = a.shape
    return a.reshape(n_b, b // n_b, t, d).transpose(0, 2, 1, 3).reshape(n_b, t * (b // n_b), d)


def _from_time_major(a, t):
    n_b, rows, d = a.shape
    bb = rows // t
    return a.reshape(n_b, t, bb, d).transpose(0, 2, 1, 3).reshape(n_b * bb, t, d)


def kernel(x_prompt, x_sample, state_rglru_h, state_rglru_conv, state_conformer_conv, meta_tokens, norm1_g, w_in, rnn_conv_w, rnn_conv_b, w_rg_a, b_rg_a, w_rg_i, b_rg_i, rg_lambda, w_rnn_proj, conf_conv_w, conf_conv_b, conf_ln_g, conf_ln_b, w_conv_proj, w_out, norm2_g, w_group, b_group, w_erouter, b_erouter, w_exp_gate, w_exp_up, w_exp_down, final_norm_g):
    assert w_in.shape[0] == 1, "single-layer trunk"
    bsz, seq, _ = x_prompt.shape
    dec_b, dec_t, _ = x_sample.shape

    vec_rows = [norm1_g[0], rnn_conv_b[0], b_rg_a[0], b_rg_i[0], rg_lambda[0], conf_conv_b[0], conf_ln_g[0],
                conf_ln_b[0], norm2_g[0]]
    vec = jnp.zeros((N_VEC_ROWS, D), F32).at[:len(vec_rows)].set(jnp.stack(vec_rows).astype(F32))
    cw4 = jnp.zeros((SUBLANES, D), F32).at[:RNN_CONV_W].set(rnn_conv_w[0])
    cw31 = jnp.zeros((4 * SUBLANES, D), F32).at[:CONF_CONV_W].set(conf_conv_w[0])
    w_router = jnp.zeros((D, LANES), F32)
    w_router = w_router.at[:, :N_GROUPS].set(w_group[0]).at[:, N_GROUPS:N_GROUPS + N_EXPERTS].set(w_erouter[0])
    wr_hi = w_router.astype(BF16)
    wr_lo = (w_router - wr_hi.astype(F32)).astype(BF16)
    b_router = jnp.zeros((1, LANES), F32)
    b_router = b_router.at[0, :N_GROUPS].set(b_group[0]).at[0, N_GROUPS:N_GROUPS + N_EXPERTS].set(
        b_erouter[0].reshape(-1))
    wts = (vec, cw4, cw31, w_in[0].astype(BF16), _block_diag_gates(w_rg_a[0], w_rg_i[0]),
           w_rnn_proj[0].astype(BF16), w_conv_proj[0].astype(BF16), w_out[0].astype(BF16), wr_hi, wr_lo, b_router)
    wg = w_exp_gate[0].astype(BF16)
    wu = w_exp_up[0].astype(BF16)
    wd = w_exp_down[0].astype(BF16).reshape(N_GROUPS, EXPERTS_PER_GROUP * D_EXPERT, D)
    gains = jnp.zeros((SUBLANES, D), F32).at[0].set(norm2_g[0]).at[1].set(final_norm_g)

    n_p = bsz * seq
    n_s = dec_b * dec_t

    meta_tm = jnp.broadcast_to(meta_tokens[:, None, :], (N_META, bsz, D)).reshape(1, N_META * bsz, D)
    _, h_m, c4_m, c31_m = _mixer_call(
        meta_tm, jnp.zeros((1, bsz, D), F32), jnp.zeros((1, (RNN_CONV_W - 1) * bsz, D), F32),
        jnp.zeros((1, (CONF_CONV_W - 1) * bsz, D), F32), wts, t_tile=N_META, b_blk=bsz)

    tok_p, h_p, c4_p, c31_p = _mixer_call(
        x_prompt, h_m, c4_m, c31_m, wts, t_tile=32, b_blk=bsz, batch_major=True)

    n_b = 4
    bb = dec_b // n_b
    tok_s, h_s, c4_s, c31_s = _mixer_call(
        _to_time_major(x_sample, n_b), state_rglru_h[0].reshape(n_b, bb, D),
        _to_time_major(state_rglru_conv[0], n_b), _to_time_major(state_conformer_conv[0], n_b), wts,
        t_tile=dec_t, b_blk=bb)

    n_tile_cap = (n_p + n_s) // MOE_TILE + N_GROUPS
    group = jnp.concatenate([tok_p[:, D], tok_s[:, D]]).astype(jnp.int32)
    slot, tile_group, n_tiles, pad0, padn = _routing_tables(group, n_tile_cap)
    sorted_tokens = _dispatch_call(tok_p, tok_s, slot, pad0, padn, n_sorted=n_tile_cap * MOE_TILE)
    y_sorted = _moe_call(sorted_tokens, tile_group, n_tiles, wg, wu, wd, gains)
    src_p = slot[:n_p].reshape(seq, bsz).T.reshape(-1)
    src_s = slot[n_p:].reshape(n_b, dec_t, bb).transpose(0, 2, 1).reshape(-1)
    y_prompt = _collect_call(y_sorted, src_p).reshape(bsz, seq, D)
    y_sample = _collect_call(y_sorted, src_s).reshape(dec_b, dec_t, D)

    return (y_prompt, y_sample,
            h_p.reshape(1, bsz, D), _from_time_major(c4_p, RNN_CONV_W - 1)[None],
            _from_time_major(c31_p, CONF_CONV_W - 1)[None],
            h_s.reshape(1, dec_b, D), _from_time_major(c4_s, RNN_CONV_W - 1)[None],
            _from_time_major(c31_s, CONF_CONV_W - 1)[None])
```

```python
import functools

import jax
import jax.numpy as jnp
from jax import lax
from jax.experimental import pallas as pl
from jax.experimental.pallas import tpu as pltpu

D = 1024
N_META = 16
N_HEADS = 16
HEAD_DIM = D // N_HEADS
RNN_CONV_W = 4
CONF_CONV_W = 31
LRU_C = 8.0
N_GROUPS = 4
EXPERTS_PER_GROUP = 8
N_EXPERTS = N_GROUPS * EXPERTS_PER_GROUP
D_EXPERT = D // 4
EPS = 1e-6

SUBLANES = 8
PACKED_ROWS = 16
LANES = 128
N_COLS = D // LANES
MXU_DIM = 256
HEADS_PER_BLOCK = MXU_DIM // HEAD_DIM
N_GATE_BLOCKS = D // MXU_DIM
VMEM_LIMIT = 56 * 1024 * 1024
TOKEN_W = D + LANES
EXPERT_LANE0 = N_GROUPS
PROMPT_T_TILE = 32
SAMPLE_BATCH_BLOCKS = 4
MOE_TILE = 256
COPY_TILE = 2048
STATS_UNROLL = 16
N_PAD_REGIONS = N_GROUPS + 1

F32 = jnp.float32
BF16 = jnp.bfloat16

(V_NORM1, V_CONV4_B, V_BA, V_BI, V_LAMBDA, V_CONV31_B, V_LN_G, V_LN_B, V_NORM2) = range(9)
N_VEC_ROWS = 16


def _dot(a, b):
    return jnp.dot(a, b, preferred_element_type=F32)


def _sigmoid(x):
    return 0.5 * jnp.tanh(0.5 * x) + 0.5


def _on_lanes(col):
    return jnp.broadcast_to(col, (col.shape[0], LANES))


def _across(tile):
    return jnp.concatenate([tile] * N_COLS, axis=1)


def _row_loop(n_rows, chunk, body, unroll):
    n = n_rows // chunk
    assert n * chunk == n_rows

    def wrapped(s, carry):
        body(pl.multiple_of(s * chunk, chunk))
        return carry
    lax.fori_loop(0, n, wrapped, 0, unroll=min(unroll, n))


def _route_info(lg):
    neg = float("-inf")
    lane = lax.broadcasted_iota(jnp.int32, lg.shape, 1).astype(F32)
    no_lane = float(LANES)

    def first_lane_of_max(vals, vmax):
        return jnp.min(jnp.where(vals == vmax, lane, no_lane), axis=-1, keepdims=True)

    gmask = lane < N_GROUPS
    gl = jnp.where(gmask, lg, neg)
    gmax = jnp.max(gl, axis=-1, keepdims=True)
    gidx = first_lane_of_max(gl, gmax)
    gsum = jnp.sum(jnp.where(gmask, jnp.exp(gl - gmax), 0.0), axis=-1, keepdims=True)
    g_gate = 1.0 / gsum
    first = EXPERT_LANE0 + gidx * EXPERTS_PER_GROUP
    emask = (lane >= first) & (lane < first + EXPERTS_PER_GROUP)
    el = jnp.where(emask, lg, neg)
    v1 = jnp.max(el, axis=-1, keepdims=True)
    i1 = first_lane_of_max(el, v1)
    el2 = jnp.where(lane == i1, neg, el)
    v2 = jnp.max(el2, axis=-1, keepdims=True)
    i2 = first_lane_of_max(el2, v2)
    e2 = jnp.exp(v2 - v1)
    w1 = 1.0 / (1.0 + e2)
    w2 = e2 / (1.0 + e2)
    comb = jnp.where(lane == i1, w1 * g_gate, 0.0) + jnp.where(lane == i2, w2 * g_gate, 0.0)
    return jnp.where(lane == 0.0, gidx, comb)


def _mixer_kernel(x_ref, h0_ref, c4_ref, c31_ref, vec_ref, cw4_ref, cw31_ref,
                  w_in_ref, w_gate_ref, w_rnn_ref, w_conv_ref, w_out_ref, wr_ref, br_ref,
                  x1c_ref, h_out_ref, c4_out_ref, c31_out_ref,
                  xn_b, ux_s, uy_s, cv_s, cg_s, grnn_s, gconv_s, glu_s, xc_s, xc_b, ga_s, gi_s, hy_s, cn_b, h_s,
                  x_s, st_s, st2_s, *, t_tile, b_blk, batch_major):
    i = pl.program_id(1)
    rows = t_tile * b_blk
    p4 = (RNN_CONV_W - 1) * b_blk
    p31 = (CONF_CONV_W - 1) * b_blk

    @pl.when(i == 0)
    def _():
        h_s[...] = h0_ref[...]
        ux_s[0:p4, :] = c4_ref[...]
        glu_s[0:p31, :] = c31_ref[...]

    def pk(r0):
        return pl.ds(r0, PACKED_ROWS)

    def vec(row):
        return vec_ref[row:row + 1, :]

    def x_rows(r0):
        if not batch_major:
            return x_ref[pk(r0), :]
        t = r0 // b_blk
        return jnp.concatenate([x_ref[:, t + k, :] for k in range(PACKED_ROWS // b_blk)], axis=0)

    def norm1_stats(r0):
        x = x_rows(r0)
        x_s[pk(r0), :] = x
        ms = jnp.mean(x * x, axis=-1, keepdims=True)
        st_s[pk(r0), :] = _on_lanes(lax.rsqrt(ms + EPS))
    _row_loop(rows, PACKED_ROWS, norm1_stats, unroll=STATS_UNROLL)

    def norm1_apply(r0):
        xn_b[pk(r0), :] = (x_s[pk(r0), :] * _across(st_s[pk(r0), :]) * vec(V_NORM1)).astype(BF16)
    _row_loop(rows, PACKED_ROWS, norm1_apply, unroll=4)

    def in_proj(group):
        return _dot(xn_b[...], w_in_ref[:, group * D:(group + 1) * D])
    ux_s[p4:p4 + rows, :] = in_proj(0)
    uy_s[...] = in_proj(1)
    cv_s[...] = in_proj(2)
    cg_s[...] = in_proj(3)
    grnn_s[...] = in_proj(4)
    gconv_s[...] = in_proj(5)

    def conv4_glu(r0):
        acc = ux_s[pk(r0), :] * cw4_ref[0:1, :]
        for k in range(1, RNN_CONV_W):
            acc = acc + ux_s[pk(r0 + k * b_blk), :] * cw4_ref[k:k + 1, :]
        xc = acc + vec(V_CONV4_B)
        xc_s[pk(r0), :] = xc
        xc_b[pk(r0), :] = xc.astype(BF16)
        glu_s[pk(p31 + r0), :] = cv_s[pk(r0), :] * _sigmoid(cg_s[pk(r0), :])
    _row_loop(rows, PACKED_ROWS, conv4_glu, unroll=2)

    strip_t = min(SUBLANES, t_tile)
    n_strip_t = t_tile // strip_t
    assert n_strip_t * strip_t == t_tile

    def conv31_strip(q, carry):
        jb = q // n_strip_t
        tc = q - jb * n_strip_t
        r0 = pl.multiple_of(tc * (strip_t * b_blk) + jb * SUBLANES, SUBLANES)
        for col in range(N_COLS):
            cs = slice(col * LANES, (col + 1) * LANES)
            xs = [glu_s[pl.ds(r0 + j * b_blk, SUBLANES), cs] for j in range(strip_t + CONF_CONV_W - 1)]
            accs = [None] * strip_t
            for k in range(CONF_CONV_W):
                w = cw31_ref[k:k + 1, cs]
                for o in range(strip_t):
                    term = xs[o + k] * w
                    accs[o] = term if k == 0 else accs[o] + term
            bias = vec_ref[V_CONV31_B:V_CONV31_B + 1, cs]
            for o in range(strip_t):
                cv_s[pl.ds(r0 + o * b_blk, SUBLANES), cs] = accs[o] + bias
        return carry
    lax.fori_loop(0, (b_blk // SUBLANES) * n_strip_t, conv31_strip, 0)

    for blk in range(N_GATE_BLOCKS):
        cols = slice(blk * MXU_DIM, (blk + 1) * MXU_DIM)
        g = _dot(xc_b[:, cols], w_gate_ref[blk])
        ga_s[:, cols] = g[:, :MXU_DIM]
        gi_s[:, cols] = g[:, MXU_DIM:]

    c_lam = -LRU_C * jax.nn.softplus(-vec(V_LAMBDA))

    def lru_coeffs(r0):
        r = _sigmoid(ga_s[pk(r0), :] + vec(V_BA))
        ig = _sigmoid(gi_s[pk(r0), :] + vec(V_BI))
        log_a = c_lam * r
        a = jnp.exp(log_a)
        mult = jnp.sqrt(-jnp.tanh(log_a) * (a * a + 1.0))
        ga_s[pk(r0), :] = a
        gi_s[pk(r0), :] = mult * (ig * xc_s[pk(r0), :])
        uy_s[pk(r0), :] = jax.nn.gelu(uy_s[pk(r0), :])
    _row_loop(rows, PACKED_ROWS, lru_coeffs, unroll=2)

    def scan_batch_slab(jb, carry):
        b0 = pl.multiple_of(jb * SUBLANES, SUBLANES)

        def step(t, h):
            sl = pl.ds(pl.multiple_of(t * b_blk + b0, SUBLANES), SUBLANES)
            h = ga_s[sl, :] * h + gi_s[sl, :]
            hy_s[sl, :] = h * uy_s[sl, :]
            return h

        h_s[pl.ds(b0, SUBLANES), :] = lax.fori_loop(0, t_tile, step, h_s[pl.ds(b0, SUBLANES), :],
                                                    unroll=min(8, t_tile))
        return carry
    lax.fori_loop(0, b_blk // SUBLANES, scan_batch_slab, 0)

    def ln_stats(r0):
        cc = cv_s[pk(r0), :]
        mu = jnp.mean(cc, axis=-1, keepdims=True)
        cen = cc - mu
        var = jnp.mean(cen * cen, axis=-1, keepdims=True)
        st_s[pk(r0), :] = _on_lanes(mu)
        st2_s[pk(r0), :] = _on_lanes(lax.rsqrt(var + EPS))
    _row_loop(rows, PACKED_ROWS, ln_stats, unroll=STATS_UNROLL)

    def ln_silu(r0):
        cen = cv_s[pk(r0), :] - _across(st_s[pk(r0), :])
        y = cen * _across(st2_s[pk(r0), :]) * vec(V_LN_G) + vec(V_LN_B)
        cn_b[pk(r0), :] = (y * _sigmoid(y)).astype(BF16)
    _row_loop(rows, PACKED_ROWS, ln_silu, unroll=4)

    ga_s[...] = _dot(hy_s[...].astype(BF16), w_rnn_ref[...])
    gi_s[...] = _dot(cn_b[...], w_conv_ref[...])

    def merge(r0):
        xn_b[pk(r0), :] = (_sigmoid(grnn_s[pk(r0), :]) * ga_s[pk(r0), :]
                           + _sigmoid(gconv_s[pk(r0), :]) * gi_s[pk(r0), :]).astype(BF16)
    _row_loop(rows, PACKED_ROWS, merge, unroll=2)
    cg_s[...] = _dot(xn_b[...], w_out_ref[...])

    def residual_stats(r0):
        x1 = x_s[pk(r0), :] + cg_s[pk(r0), :]
        x1c_ref[pk(r0), 0:D] = x1
        ms = jnp.mean(x1 * x1, axis=-1, keepdims=True)
        st_s[pk(r0), :] = _on_lanes(lax.rsqrt(ms + EPS))
    _row_loop(rows, PACKED_ROWS, residual_stats, unroll=STATS_UNROLL)

    def norm2_split(r0):
        xn2 = x1c_ref[pk(r0), 0:D] * _across(st_s[pk(r0), :]) * vec(V_NORM2)
        hi = xn2.astype(BF16)
        cn_b[pk(r0), :] = hi
        xc_b[pk(r0), :] = (xn2 - hi.astype(F32)).astype(BF16)
    _row_loop(rows, PACKED_ROWS, norm2_split, unroll=4)

    hi_prod = _dot(cn_b[...], wr_ref[...])
    logits = hi_prod[:, :LANES] + hi_prod[:, LANES:] + _dot(xc_b[...], wr_ref[:, :LANES]) + br_ref[...]
    x1c_ref[:, D:D + LANES] = _route_info(logits)

    for k in range(RNN_CONV_W - 1):
        ux_s[k * b_blk:(k + 1) * b_blk, :] = ux_s[(t_tile + k) * b_blk:(t_tile + k + 1) * b_blk, :]
    for k in range(CONF_CONV_W - 1):
        glu_s[k * b_blk:(k + 1) * b_blk, :] = glu_s[(t_tile + k) * b_blk:(t_tile + k + 1) * b_blk, :]

    @pl.when(i == pl.num_programs(1) - 1)
    def _():
        h_out_ref[...] = h_s[...]
        c4_out_ref[...] = ux_s[0:p4, :]
        c31_out_ref[...] = glu_s[0:p31, :]


def _mixer_call(x, h0, c4, c31, wts, *, t_tile, b_blk, batch_major=False):
    if batch_major:
        assert x.shape[0] == b_blk and PACKED_ROWS % b_blk == 0
        n_b, total_rows = 1, x.shape[0] * x.shape[1]
    else:
        n_b, total_rows, _ = x.shape
    rows = t_tile * b_blk
    n_t = total_rows // rows
    assert n_t * rows == total_rows and b_blk % SUBLANES == 0 and rows % PACKED_ROWS == 0
    assert t_tile >= RNN_CONV_W - 1
    p4 = (RNN_CONV_W - 1) * b_blk
    p31 = (CONF_CONV_W - 1) * b_blk

    def tile_spec(width):
        return pl.BlockSpec((None, rows, width), lambda j, i: (j, i, 0))

    def state_spec(n_rows):
        return pl.BlockSpec((None, n_rows, D), lambda j, i: (j, 0, 0))

    def const_spec(arr):
        nd = arr.ndim
        return pl.BlockSpec(arr.shape, lambda j, i: (0,) * nd, pipeline_mode=pl.Buffered(1))

    plane = pltpu.VMEM((rows, D), F32)
    plane_b = pltpu.VMEM((rows, D), BF16)
    return pl.pallas_call(
        functools.partial(_mixer_kernel, t_tile=t_tile, b_blk=b_blk, batch_major=batch_major),
        grid=(n_b, n_t),
        in_specs=[pl.BlockSpec((b_blk, t_tile, D), lambda j, i: (0, i, 0)) if batch_major else tile_spec(D),
                  state_spec(b_blk), state_spec(p4), state_spec(p31)]
                 + [const_spec(w) for w in wts],
        out_specs=[pl.BlockSpec((rows, TOKEN_W), lambda j, i: (j * n_t + i, 0)),
                   state_spec(b_blk), state_spec(p4), state_spec(p31)],
        out_shape=[jax.ShapeDtypeStruct((n_b * total_rows, TOKEN_W), F32),
                   jax.ShapeDtypeStruct((n_b, b_blk, D), F32),
                   jax.ShapeDtypeStruct((n_b, p4, D), F32),
                   jax.ShapeDtypeStruct((n_b, p31, D), F32)],
        scratch_shapes=[plane_b, pltpu.VMEM((p4 + rows, D), F32), plane, plane, plane, plane, plane,
                        pltpu.VMEM((p31 + rows, D), F32), plane, plane_b, plane, plane, plane, plane_b,
                        pltpu.VMEM((b_blk, D), F32), plane, pltpu.VMEM((rows, LANES), F32),
                        pltpu.VMEM((rows, LANES), F32)],
        compiler_params=pltpu.CompilerParams(dimension_semantics=("arbitrary", "arbitrary"),
                                             vmem_limit_bytes=VMEM_LIMIT),
        name="mixer",
    )(x, h0, c4, c31, *wts)


def _row_copies(n, make_copy):
    def issue(r, carry):
        make_copy(r).start()
        return carry
    lax.fori_loop(0, n, issue, 0, unroll=8)


def _copy_tile(n_rows):
    tile = COPY_TILE
    while n_rows % tile:
        tile //= 2
    return tile


def _dispatch_kernel(slot_ref, pad0_ref, padn_ref, tok_a_ref, tok_b_ref, sorted_hbm, zero_s, sem, pad_sem,
                     *, n_a_tiles):
    i = pl.program_id(0)
    tile_a, tile_b = tok_a_ref.shape[0], tok_b_ref.shape[0]

    def scatter(tok_ref, token0):
        n = tok_ref.shape[0]

        def row_copy(r):
            return pltpu.make_async_copy(tok_ref.at[pl.ds(r, 1), :],
                                         sorted_hbm.at[pl.ds(slot_ref[token0 + r], 1), :], sem)
        _row_copies(n, row_copy)
        pltpu.make_async_copy(tok_ref, sorted_hbm.at[pl.ds(0, n), :], sem).wait()

    @pl.when(i < n_a_tiles)
    def _():
        scatter(tok_a_ref, i * tile_a)

    @pl.when(i >= n_a_tiles)
    def _():
        scatter(tok_b_ref, n_a_tiles * tile_a + (i - n_a_tiles) * tile_b)

    @pl.when(i == pl.num_programs(0) - 1)
    def _():
        zero_s[...] = jnp.zeros_like(zero_s)
        for region in range(N_PAD_REGIONS):
            def pad_copy(k, region=region):
                return pltpu.make_async_copy(zero_s.at[pl.ds(0, 1), :],
                                             sorted_hbm.at[pl.ds(pad0_ref[region] + k, 1), :], pad_sem)

            def issue(k, carry):
                pad_copy(k).start()
                return carry

            def drain(k, carry):
                pad_copy(k).wait()
                return carry
            lax.fori_loop(0, padn_ref[region], issue, 0)
            lax.fori_loop(0, padn_ref[region], drain, 0)


def _dispatch_call(tokens_a, tokens_b, slot, pad0, padn, *, n_sorted):
    n_a, n_b = tokens_a.shape[0], tokens_b.shape[0]
    tile_a, tile_b = _copy_tile(n_a), _copy_tile(n_b)
    n_a_tiles = n_a // tile_a
    return pl.pallas_call(
        functools.partial(_dispatch_kernel, n_a_tiles=n_a_tiles),
        grid_spec=pltpu.PrefetchScalarGridSpec(
            num_scalar_prefetch=3, grid=(n_a_tiles + n_b // tile_b,),
            in_specs=[pl.BlockSpec((tile_a, TOKEN_W), lambda i, *_: (jnp.minimum(i, n_a_tiles - 1), 0)),
                      pl.BlockSpec((tile_b, TOKEN_W), lambda i, *_: (jnp.maximum(i - n_a_tiles, 0), 0))],
            out_specs=pl.BlockSpec(memory_space=pl.ANY),
            scratch_shapes=[pltpu.VMEM((SUBLANES, TOKEN_W), F32), pltpu.SemaphoreType.DMA(()),
                            pltpu.SemaphoreType.DMA(())]),
        out_shape=jax.ShapeDtypeStruct((n_sorted, TOKEN_W), F32),
        compiler_params=pltpu.CompilerParams(dimension_semantics=("arbitrary",), vmem_limit_bytes=VMEM_LIMIT),
        name="moe_dispatch",
    )(slot, pad0, padn, tokens_a, tokens_b)


def _moe_kernel(tile_group_ref, n_tiles_ref, tok_ref, wg_ref, wu_ref, wd_ref, gains_ref, y_ref,
                xn_b, he_b, y_s, st_s):
    i = pl.program_id(0)
    live = i < n_tiles_ref[0]

    def pk(r0):
        return pl.ds(r0, PACKED_ROWS)

    @pl.when(live)
    def _():
        def norm2_stats(r0):
            x1 = tok_ref[pk(r0), 0:D]
            ms = jnp.mean(x1 * x1, axis=-1, keepdims=True)
            st_s[pk(r0), :] = _on_lanes(lax.rsqrt(ms + EPS))
        _row_loop(MOE_TILE, PACKED_ROWS, norm2_stats, unroll=STATS_UNROLL)

        def norm2(r0):
            xn_b[pk(r0), :] = (tok_ref[pk(r0), 0:D] * _across(st_s[pk(r0), :]) * gains_ref[0:1, :]).astype(BF16)
        _row_loop(MOE_TILE, PACKED_ROWS, norm2, unroll=4)

        info = tok_ref[:, D:D + LANES]
        lane = lax.broadcasted_iota(jnp.int32, info.shape, 1)
        lane0 = EXPERT_LANE0 + tile_group_ref[i] * EXPERTS_PER_GROUP
        for e in range(EXPERTS_PER_GROUP):
            scale = jnp.sum(jnp.where(lane == lane0 + e, info, 0.0), axis=-1, keepdims=True)
            gate = _dot(xn_b[...], wg_ref[e])
            he = gate * _sigmoid(gate) * _dot(xn_b[...], wu_ref[e])
            he_b[:, e * D_EXPERT:(e + 1) * D_EXPERT] = (he * scale).astype(BF16)
        y_s[...] = _dot(he_b[...], wd_ref[...])

        def residual_stats(r0):
            y = tok_ref[pk(r0), 0:D] + y_s[pk(r0), :]
            y_s[pk(r0), :] = y
            ms = jnp.mean(y * y, axis=-1, keepdims=True)
            st_s[pk(r0), :] = _on_lanes(lax.rsqrt(ms + EPS))
        _row_loop(MOE_TILE, PACKED_ROWS, residual_stats, unroll=STATS_UNROLL)

        def final_norm(r0):
            y_ref[pk(r0), :] = y_s[pk(r0), :] * _across(st_s[pk(r0), :]) * gains_ref[1:2, :]
        _row_loop(MOE_TILE, PACKED_ROWS, final_norm, unroll=4)

    @pl.when(jnp.logical_not(live))
    def _():
        y_ref[...] = jnp.zeros_like(y_ref)


def _moe_call(sorted_tokens, tile_group, n_tiles, wg, wu, wd, gains):
    n_sorted = sorted_tokens.shape[0]
    assert n_sorted % MOE_TILE == 0
    hidden = EXPERTS_PER_GROUP * D_EXPERT

    def tile(i, tile_group_ref, n_tiles_ref):
        return (jnp.minimum(i, n_tiles_ref[0] - 1), 0)

    def group(i, tile_group_ref, n_tiles_ref):
        return (tile_group_ref[jnp.minimum(i, n_tiles_ref[0] - 1)], 0, 0)

    return pl.pallas_call(
        _moe_kernel,
        grid_spec=pltpu.PrefetchScalarGridSpec(
            num_scalar_prefetch=2, grid=(n_sorted // MOE_TILE,),
            in_specs=[pl.BlockSpec((MOE_TILE, TOKEN_W), tile),
                      pl.BlockSpec((EXPERTS_PER_GROUP, D, D_EXPERT), group),
                      pl.BlockSpec((EXPERTS_PER_GROUP, D, D_EXPERT), group),
                      pl.BlockSpec((None, hidden, D), group),
                      pl.BlockSpec((SUBLANES, D), lambda i, *_: (0, 0))],
            out_specs=pl.BlockSpec((MOE_TILE, D), lambda i, *_: (i, 0)),
            scratch_shapes=[pltpu.VMEM((MOE_TILE, D), BF16), pltpu.VMEM((MOE_TILE, hidden), BF16),
                            pltpu.VMEM((MOE_TILE, D), F32), pltpu.VMEM((MOE_TILE, LANES), F32)]),
        out_shape=jax.ShapeDtypeStruct((n_sorted, D), F32),
        compiler_params=pltpu.CompilerParams(dimension_semantics=("arbitrary",),
                                             vmem_limit_bytes=VMEM_LIMIT),
        name="moe_grouped",
    )(tile_group, n_tiles, sorted_tokens, wg, wu, wd, gains)


def _collect_kernel(src_ref, y_hbm, o_ref, sem):
    i = pl.program_id(0)
    tile = o_ref.shape[0]

    def row_copy(r):
        return pltpu.make_async_copy(y_hbm.at[pl.ds(src_ref[i * tile + r], 1), :],
                                     o_ref.at[pl.ds(r, 1), :], sem)
    _row_copies(tile, row_copy)
    pltpu.make_async_copy(y_hbm.at[pl.ds(0, tile), :], o_ref, sem).wait()


def _collect_call(y_sorted, src):
    n = src.shape[0]
    tile = _copy_tile(n)
    return pl.pallas_call(
        _collect_kernel,
        grid_spec=pltpu.PrefetchScalarGridSpec(
            num_scalar_prefetch=1, grid=(n // tile,),
            in_specs=[pl.BlockSpec(memory_space=pl.ANY)],
            out_specs=pl.BlockSpec((tile, D), lambda i, *_: (i, 0)),
            scratch_shapes=[pltpu.SemaphoreType.DMA(())]),
        out_shape=jax.ShapeDtypeStruct((n, D), F32),
        compiler_params=pltpu.CompilerParams(dimension_semantics=("arbitrary",), vmem_limit_bytes=VMEM_LIMIT),
        name="moe_collect",
    )(src, y_sorted)


def _routing_tables(group, n_tile_cap):
    onehot = (group[:, None] == jnp.arange(N_GROUPS, dtype=jnp.int32)[None, :]).astype(jnp.int32)
    running = jnp.cumsum(onehot, axis=0)
    rank = jnp.sum(running * onehot, axis=1) - 1
    counts = running[-1]
    tiles_per_group = (counts + MOE_TILE - 1) // MOE_TILE
    end_tile = jnp.cumsum(tiles_per_group)
    first_row = (end_tile - tiles_per_group) * MOE_TILE
    slot = jnp.sum(first_row[None, :] * onehot, axis=1) + rank
    tile_ids = jnp.arange(n_tile_cap, dtype=jnp.int32)
    tile_group = jnp.minimum(jnp.sum((tile_ids[:, None] >= end_tile[None, :]).astype(jnp.int32), axis=1),
                             N_GROUPS - 1)
    used_rows = end_tile[-1:] * MOE_TILE
    pad0 = jnp.concatenate([first_row + counts, used_rows])
    padn = jnp.concatenate([tiles_per_group * MOE_TILE - counts, n_tile_cap * MOE_TILE - used_rows])
    return (slot.astype(jnp.int32), tile_group.astype(jnp.int32), end_tile[-1:].astype(jnp.int32),
            pad0.astype(jnp.int32), padn.astype(jnp.int32))


def _block_diag_gates(w_a, w_i):
    def bd(w):
        w = w.reshape(N_GATE_BLOCKS, HEADS_PER_BLOCK, HEAD_DIM, HEAD_DIM)
        eye = jnp.eye(HEADS_PER_BLOCK, dtype=w.dtype)
        full = jnp.einsum("bhij,hg->bhigj", w, eye)
        return full.reshape(N_GATE_BLOCKS, MXU_DIM, MXU_DIM)
    return jnp.concatenate([bd(w_a), bd(w_i)], axis=-1).astype(BF16)


def _to_time_major(a, n_b):
    b, t, d = a.shape
    return a.reshape(n_b, b // n_b, t, d).transpose(0, 2, 1, 3).reshape(n_b, t * (b // n_b), d)


def _from_time_major(a, t):
    n_b, rows, d = a.shape
    bb = rows // t
    return a.reshape(n_b, t, bb, d).transpose(0, 2, 1, 3).reshape(n_b * bb, t, d)


def kernel(x_prompt, x_sample, state_rglru_h, state_rglru_conv, state_conformer_conv, meta_tokens, norm1_g, w_in, rnn_conv_w, rnn_conv_b, w_rg_a, b_rg_a, w_rg_i, b_rg_i, rg_lambda, w_rnn_proj, conf_conv_w, conf_conv_b, conf_ln_g, conf_ln_b, w_conv_proj, w_out, norm2_g, w_group, b_group, w_erouter, b_erouter, w_exp_gate, w_exp_up, w_exp_down, final_norm_g):
    assert w_in.shape[0] == 1, "single-layer trunk"
    bsz, seq, _ = x_prompt.shape
    dec_b, dec_t, _ = x_sample.shape

    vec_rows = [norm1_g[0], rnn_conv_b[0], b_rg_a[0], b_rg_i[0], rg_lambda[0], conf_conv_b[0], conf_ln_g[0],
                conf_ln_b[0], norm2_g[0]]
    vec = jnp.zeros((N_VEC_ROWS, D), F32).at[:len(vec_rows)].set(jnp.stack(vec_rows).astype(F32))
    cw4 = jnp.zeros((SUBLANES, D), F32).at[:RNN_CONV_W].set(rnn_conv_w[0])
    cw31 = jnp.zeros((4 * SUBLANES, D), F32).at[:CONF_CONV_W].set(conf_conv_w[0])
    w_router = jnp.zeros((D, LANES), F32)
    w_router = w_router.at[:, :N_GROUPS].set(w_group[0]).at[:, N_GROUPS:N_GROUPS + N_EXPERTS].set(w_erouter[0])
    wr_hi = w_router.astype(BF16)
    wr_lo = (w_router - wr_hi.astype(F32)).astype(BF16)
    wr_split = jnp.concatenate([wr_hi, wr_lo], axis=1)
    b_router = jnp.zeros((1, LANES), F32)
    b_router = b_router.at[0, :N_GROUPS].set(b_group[0]).at[0, N_GROUPS:N_GROUPS + N_EXPERTS].set(
        b_erouter[0].reshape(-1))
    wts = (vec, cw4, cw31, w_in[0].astype(BF16), _block_diag_gates(w_rg_a[0], w_rg_i[0]),
           w_rnn_proj[0].astype(BF16), w_conv_proj[0].astype(BF16), w_out[0].astype(BF16), wr_split, b_router)
    wg = w_exp_gate[0].astype(BF16)
    wu = w_exp_up[0].astype(BF16)
    wd = w_exp_down[0].astype(BF16).reshape(N_GROUPS, EXPERTS_PER_GROUP * D_EXPERT, D)
    gains = jnp.zeros((SUBLANES, D), F32).at[0].set(norm2_g[0]).at[1].set(final_norm_g)

    n_p = bsz * seq
    n_s = dec_b * dec_t

    meta_tm = jnp.broadcast_to(meta_tokens[:, None, :], (N_META, bsz, D)).reshape(1, N_META * bsz, D)
    _, h_m, c4_m, c31_m = _mixer_call(
        meta_tm, jnp.zeros((1, bsz, D), F32), jnp.zeros((1, (RNN_CONV_W - 1) * bsz, D), F32),
        jnp.zeros((1, (CONF_CONV_W - 1) * bsz, D), F32), wts, t_tile=N_META, b_blk=bsz)

    tok_p, h_p, c4_p, c31_p = _mixer_call(
        x_prompt, h_m, c4_m, c31_m, wts, t_tile=PROMPT_T_TILE, b_blk=bsz, batch_major=True)

    n_b = SAMPLE_BATCH_BLOCKS
    bb = dec_b // n_b
    tok_s, h_s, c4_s, c31_s = _mixer_call(
        _to_time_major(x_sample, n_b), state_rglru_h[0].reshape(n_b, bb, D),
        _to_time_major(state_rglru_conv[0], n_b), _to_time_major(state_conformer_conv[0], n_b), wts,
        t_tile=dec_t, b_blk=bb)

    n_tile_cap = (n_p + n_s) // MOE_TILE + N_GROUPS
    group = jnp.concatenate([tok_p[:, D], tok_s[:, D]]).astype(jnp.int32)
    slot, tile_group, n_tiles, pad0, padn = _routing_tables(group, n_tile_cap)
    sorted_tokens = _dispatch_call(tok_p, tok_s, slot, pad0, padn, n_sorted=n_tile_cap * MOE_TILE)
    y_sorted = _moe_call(sorted_tokens, tile_group, n_tiles, wg, wu, wd, gains)
    src_p = slot[:n_p].reshape(seq, bsz).T.reshape(-1)
    src_s = slot[n_p:].reshape(n_b, dec_t, bb).transpose(0, 2, 1).reshape(-1)
    y_prompt = _collect_call(y_sorted, src_p).reshape(bsz, seq, D)
    y_sample = _collect_call(y_sorted, src_s).reshape(dec_b, dec_t, D)

    return (y_prompt, y_sample,
            h_p.reshape(1, bsz, D), _from_time_major(c4_p, RNN_CONV_W - 1)[None],
            _from_time_major(c31_p, CONF_CONV_W - 1)[None],
            h_s.reshape(1, dec_b, D), _from_time_major(c4_s, RNN_CONV_W - 1)[None],
            _from_time_major(c31_s, CONF_CONV_W - 1)[None])
```

```python
import functools

import jax
import jax.numpy as jnp
from jax import lax
from jax.experimental import pallas as pl
from jax.experimental.pallas import tpu as pltpu

D = 1024
N_META = 16
N_HEADS = 16
HEAD_DIM = D // N_HEADS
RNN_CONV_W = 4
CONF_CONV_W = 31
LRU_C = 8.0
N_GROUPS = 4
EXPERTS_PER_GROUP = 8
N_EXPERTS = N_GROUPS * EXPERTS_PER_GROUP
D_EXPERT = D // 4
EPS = 1e-6

SUBLANES = 8
PACKED_ROWS = 16
LANES = 128
N_COLS = D // LANES
MXU_DIM = 256
HEADS_PER_BLOCK = MXU_DIM // HEAD_DIM
N_GATE_BLOCKS = D // MXU_DIM
VMEM_LIMIT = 56 * 1024 * 1024
TOKEN_W = D + LANES
EXPERT_LANE0 = N_GROUPS
PROMPT_T_TILE = 32
SAMPLE_BATCH_BLOCKS = 4
MOE_TILE = 256
COPY_TILE = 4096
STATS_UNROLL = 16
N_PAD_REGIONS = N_GROUPS + 1

F32 = jnp.float32
BF16 = jnp.bfloat16

(V_NORM1, V_CONV4_B, V_BA, V_BI, V_LAMBDA, V_CONV31_B, V_LN_G, V_LN_B, V_NORM2) = range(9)
N_VEC_ROWS = 16


def _dot(a, b):
    return jnp.dot(a, b, preferred_element_type=F32)


def _sigmoid(x):
    return 0.5 * jnp.tanh(0.5 * x) + 0.5


def _on_lanes(col):
    return jnp.broadcast_to(col, (col.shape[0], LANES))


def _across(tile):
    return jnp.concatenate([tile] * N_COLS, axis=1)


def _row_loop(n_rows, chunk, body, unroll):
    n = n_rows // chunk
    assert n * chunk == n_rows

    def wrapped(s, carry):
        body(pl.multiple_of(s * chunk, chunk))
        return carry
    lax.fori_loop(0, n, wrapped, 0, unroll=min(unroll, n))


def _route_info(lg):
    neg = float("-inf")
    lane = lax.broadcasted_iota(jnp.int32, lg.shape, 1).astype(F32)
    no_lane = float(LANES)

    def first_lane_of_max(vals, vmax):
        return jnp.min(jnp.where(vals == vmax, lane, no_lane), axis=-1, keepdims=True)

    gmask = lane < N_GROUPS
    gl = jnp.where(gmask, lg, neg)
    gmax = jnp.max(gl, axis=-1, keepdims=True)
    gidx = first_lane_of_max(gl, gmax)
    gsum = jnp.sum(jnp.where(gmask, jnp.exp(gl - gmax), 0.0), axis=-1, keepdims=True)
    g_gate = 1.0 / gsum
    first = EXPERT_LANE0 + gidx * EXPERTS_PER_GROUP
    emask = (lane >= first) & (lane < first + EXPERTS_PER_GROUP)
    el = jnp.where(emask, lg, neg)
    v1 = jnp.max(el, axis=-1, keepdims=True)
    i1 = first_lane_of_max(el, v1)
    el2 = jnp.where(lane == i1, neg, el)
    v2 = jnp.max(el2, axis=-1, keepdims=True)
    i2 = first_lane_of_max(el2, v2)
    e2 = jnp.exp(v2 - v1)
    w1 = 1.0 / (1.0 + e2)
    w2 = e2 / (1.0 + e2)
    comb = jnp.where(lane == i1, w1 * g_gate, 0.0) + jnp.where(lane == i2, w2 * g_gate, 0.0)
    return jnp.where(lane == 0.0, gidx, comb)


def _mixer_kernel(x_ref, h0_ref, c4_ref, c31_ref, vec_ref, cw4_ref, cw31_ref,
                  w_in_ref, w_gate_ref, w_rnn_ref, w_conv_ref, w_out_ref, wr_ref, br_ref,
                  x1c_ref, h_out_ref, c4_out_ref, c31_out_ref,
                  xn_b, ux_s, uy_s, cv_s, cg_s, grnn_s, gconv_s, glu_s, xc_s, xc_b, ga_s, gi_s, hy_s, cn_b, h_s,
                  x_s, st_s, st2_s, *, t_tile, b_blk, batch_major):
    i = pl.program_id(1)
    rows = t_tile * b_blk
    p4 = (RNN_CONV_W - 1) * b_blk
    p31 = (CONF_CONV_W - 1) * b_blk

    @pl.when(i == 0)
    def _():
        h_s[...] = h0_ref[...]
        ux_s[0:p4, :] = c4_ref[...]
        glu_s[0:p31, :] = c31_ref[...]

    def pk(r0):
        return pl.ds(r0, PACKED_ROWS)

    def vec(row):
        return vec_ref[row:row + 1, :]

    def x_rows(r0):
        if not batch_major:
            return x_ref[pk(r0), :]
        t = r0 // b_blk
        return jnp.concatenate([x_ref[:, t + k, :] for k in range(PACKED_ROWS // b_blk)], axis=0)

    def norm1_stats(r0):
        x = x_rows(r0)
        x_s[pk(r0), :] = x
        ms = jnp.mean(x * x, axis=-1, keepdims=True)
        st_s[pk(r0), :] = _on_lanes(lax.rsqrt(ms + EPS))
    _row_loop(rows, PACKED_ROWS, norm1_stats, unroll=STATS_UNROLL)

    def norm1_apply(r0):
        xn_b[pk(r0), :] = (x_s[pk(r0), :] * _across(st_s[pk(r0), :]) * vec(V_NORM1)).astype(BF16)
    _row_loop(rows, PACKED_ROWS, norm1_apply, unroll=4)

    def in_proj(group):
        return _dot(xn_b[...], w_in_ref[:, group * D:(group + 1) * D])
    ux_s[p4:p4 + rows, :] = in_proj(0)
    uy_s[...] = in_proj(1)
    cv_s[...] = in_proj(2)
    cg_s[...] = in_proj(3)
    grnn_s[...] = in_proj(4)
    gconv_s[...] = in_proj(5)

    def conv4_glu(r0):
        acc = ux_s[pk(r0), :] * cw4_ref[0:1, :]
        for k in range(1, RNN_CONV_W):
            acc = acc + ux_s[pk(r0 + k * b_blk), :] * cw4_ref[k:k + 1, :]
        xc = acc + vec(V_CONV4_B)
        xc_s[pk(r0), :] = xc
        xc_b[pk(r0), :] = xc.astype(BF16)
        glu_s[pk(p31 + r0), :] = cv_s[pk(r0), :] * _sigmoid(cg_s[pk(r0), :])
    _row_loop(rows, PACKED_ROWS, conv4_glu, unroll=2)

    strip_t = min(SUBLANES, t_tile)
    n_strip_t = t_tile // strip_t
    assert n_strip_t * strip_t == t_tile

    def conv31_strip(q, carry):
        jb = q // n_strip_t
        tc = q - jb * n_strip_t
        r0 = pl.multiple_of(tc * (strip_t * b_blk) + jb * SUBLANES, SUBLANES)
        for col in range(N_COLS):
            cs = slice(col * LANES, (col + 1) * LANES)
            xs = [glu_s[pl.ds(r0 + j * b_blk, SUBLANES), cs] for j in range(strip_t + CONF_CONV_W - 1)]
            accs = [None] * strip_t
            for k in range(CONF_CONV_W):
                w = cw31_ref[k:k + 1, cs]
                for o in range(strip_t):
                    term = xs[o + k] * w
                    accs[o] = term if k == 0 else accs[o] + term
            bias = vec_ref[V_CONV31_B:V_CONV31_B + 1, cs]
            for o in range(strip_t):
                cv_s[pl.ds(r0 + o * b_blk, SUBLANES), cs] = accs[o] + bias
        return carry
    lax.fori_loop(0, (b_blk // SUBLANES) * n_strip_t, conv31_strip, 0)

    for blk in range(N_GATE_BLOCKS):
        cols = slice(blk * MXU_DIM, (blk + 1) * MXU_DIM)
        g = _dot(xc_b[:, cols], w_gate_ref[blk])
        ga_s[:, cols] = g[:, :MXU_DIM]
        gi_s[:, cols] = g[:, MXU_DIM:]

    c_lam = -LRU_C * jax.nn.softplus(-vec(V_LAMBDA))

    def lru_coeffs(r0):
        r = _sigmoid(ga_s[pk(r0), :] + vec(V_BA))
        ig = _sigmoid(gi_s[pk(r0), :] + vec(V_BI))
        log_a = c_lam * r
        a = jnp.exp(log_a)
        mult = jnp.sqrt(-jnp.tanh(log_a) * (a * a + 1.0))
        ga_s[pk(r0), :] = a
        gi_s[pk(r0), :] = mult * (ig * xc_s[pk(r0), :])
        uy_s[pk(r0), :] = jax.nn.gelu(uy_s[pk(r0), :])
    _row_loop(rows, PACKED_ROWS, lru_coeffs, unroll=2)

    def scan_batch_slab(jb, carry):
        b0 = pl.multiple_of(jb * SUBLANES, SUBLANES)

        def step(t, h):
            sl = pl.ds(pl.multiple_of(t * b_blk + b0, SUBLANES), SUBLANES)
            h = ga_s[sl, :] * h + gi_s[sl, :]
            hy_s[sl, :] = h * uy_s[sl, :]
            return h

        h_s[pl.ds(b0, SUBLANES), :] = lax.fori_loop(0, t_tile, step, h_s[pl.ds(b0, SUBLANES), :],
                                                    unroll=min(8, t_tile))
        return carry
    lax.fori_loop(0, b_blk // SUBLANES, scan_batch_slab, 0)

    def ln_stats(r0):
        cc = cv_s[pk(r0), :]
        mu = jnp.mean(cc, axis=-1, keepdims=True)
        cen = cc - mu
        var = jnp.mean(cen * cen, axis=-1, keepdims=True)
        st_s[pk(r0), :] = _on_lanes(mu)
        st2_s[pk(r0), :] = _on_lanes(lax.rsqrt(var + EPS))
    _row_loop(rows, PACKED_ROWS, ln_stats, unroll=STATS_UNROLL)

    def ln_silu(r0):
        cen = cv_s[pk(r0), :] - _across(st_s[pk(r0), :])
        y = cen * _across(st2_s[pk(r0), :]) * vec(V_LN_G) + vec(V_LN_B)
        cn_b[pk(r0), :] = (y * _sigmoid(y)).astype(BF16)
    _row_loop(rows, PACKED_ROWS, ln_silu, unroll=4)

    ga_s[...] = _dot(hy_s[...].astype(BF16), w_rnn_ref[...])
    gi_s[...] = _dot(cn_b[...], w_conv_ref[...])

    def merge(r0):
        xn_b[pk(r0), :] = (_sigmoid(grnn_s[pk(r0), :]) * ga_s[pk(r0), :]
                           + _sigmoid(gconv_s[pk(r0), :]) * gi_s[pk(r0), :]).astype(BF16)
    _row_loop(rows, PACKED_ROWS, merge, unroll=2)
    cg_s[...] = _dot(xn_b[...], w_out_ref[...])

    def residual_stats(r0):
        x1 = x_s[pk(r0), :] + cg_s[pk(r0), :]
        x1c_ref[pk(r0), 0:D] = x1
        ms = jnp.mean(x1 * x1, axis=-1, keepdims=True)
        st_s[pk(r0), :] = _on_lanes(lax.rsqrt(ms + EPS))
    _row_loop(rows, PACKED_ROWS, residual_stats, unroll=STATS_UNROLL)

    def norm2_split(r0):
        xn2 = x1c_ref[pk(r0), 0:D] * _across(st_s[pk(r0), :]) * vec(V_NORM2)
        hi = xn2.astype(BF16)
        cn_b[pk(r0), :] = hi
        xc_b[pk(r0), :] = (xn2 - hi.astype(F32)).astype(BF16)
    _row_loop(rows, PACKED_ROWS, norm2_split, unroll=4)

    hi_prod = _dot(cn_b[...], wr_ref[...])
    logits = hi_prod[:, :LANES] + hi_prod[:, LANES:] + _dot(xc_b[...], wr_ref[:, :LANES]) + br_ref[...]
    x1c_ref[:, D:D + LANES] = _route_info(logits)

    for k in range(RNN_CONV_W - 1):
        ux_s[k * b_blk:(k + 1) * b_blk, :] = ux_s[(t_tile + k) * b_blk:(t_tile + k + 1) * b_blk, :]
    for k in range(CONF_CONV_W - 1):
        glu_s[k * b_blk:(k + 1) * b_blk, :] = glu_s[(t_tile + k) * b_blk:(t_tile + k + 1) * b_blk, :]

    @pl.when(i == pl.num_programs(1) - 1)
    def _():
        h_out_ref[...] = h_s[...]
        c4_out_ref[...] = ux_s[0:p4, :]
        c31_out_ref[...] = glu_s[0:p31, :]


def _mixer_call(x, h0, c4, c31, wts, *, t_tile, b_blk, batch_major=False):
    if batch_major:
        assert x.shape[0] == b_blk and PACKED_ROWS % b_blk == 0
        n_b, total_rows = 1, x.shape[0] * x.shape[1]
    else:
        n_b, total_rows, _ = x.shape
    rows = t_tile * b_blk
    n_t = total_rows // rows
    assert n_t * rows == total_rows and b_blk % SUBLANES == 0 and rows % PACKED_ROWS == 0
    assert t_tile >= RNN_CONV_W - 1
    p4 = (RNN_CONV_W - 1) * b_blk
    p31 = (CONF_CONV_W - 1) * b_blk

    def tile_spec(width):
        return pl.BlockSpec((None, rows, width), lambda j, i: (j, i, 0))

    def state_spec(n_rows):
        return pl.BlockSpec((None, n_rows, D), lambda j, i: (j, 0, 0))

    def const_spec(arr):
        nd = arr.ndim
        return pl.BlockSpec(arr.shape, lambda j, i: (0,) * nd, pipeline_mode=pl.Buffered(1))

    plane = pltpu.VMEM((rows, D), F32)
    plane_b = pltpu.VMEM((rows, D), BF16)
    return pl.pallas_call(
        functools.partial(_mixer_kernel, t_tile=t_tile, b_blk=b_blk, batch_major=batch_major),
        grid=(n_b, n_t),
        in_specs=[pl.BlockSpec((b_blk, t_tile, D), lambda j, i: (0, i, 0)) if batch_major else tile_spec(D),
                  state_spec(b_blk), state_spec(p4), state_spec(p31)]
                 + [const_spec(w) for w in wts],
        out_specs=[pl.BlockSpec((rows, TOKEN_W), lambda j, i: (j * n_t + i, 0)),
                   state_spec(b_blk), state_spec(p4), state_spec(p31)],
        out_shape=[jax.ShapeDtypeStruct((n_b * total_rows, TOKEN_W), F32),
                   jax.ShapeDtypeStruct((n_b, b_blk, D), F32),
                   jax.ShapeDtypeStruct((n_b, p4, D), F32),
                   jax.ShapeDtypeStruct((n_b, p31, D), F32)],
        scratch_shapes=[plane_b, pltpu.VMEM((p4 + rows, D), F32), plane, plane, plane, plane, plane,
                        pltpu.VMEM((p31 + rows, D), F32), plane, plane_b, plane, plane, plane, plane_b,
                        pltpu.VMEM((b_blk, D), F32), plane, pltpu.VMEM((rows, LANES), F32),
                        pltpu.VMEM((rows, LANES), F32)],
        compiler_params=pltpu.CompilerParams(dimension_semantics=("arbitrary", "arbitrary"),
                                             vmem_limit_bytes=VMEM_LIMIT),
        name="mixer",
    )(x, h0, c4, c31, *wts)


def _row_copies(n, make_copy):
    def issue(r, carry):
        make_copy(r).start()
        return carry
    lax.fori_loop(0, n, issue, 0, unroll=8)


def _copy_tile(n_rows):
    tile = COPY_TILE
    while n_rows % tile:
        tile //= 2
    return tile


def _dispatch_kernel(slot_ref, pad0_ref, padn_ref, tok_a_ref, tok_b_ref, sorted_hbm, zero_s, sem, pad_sem,
                     *, n_a_tiles):
    i = pl.program_id(0)
    tile_a, tile_b = tok_a_ref.shape[0], tok_b_ref.shape[0]

    def scatter(tok_ref, token0):
        n = tok_ref.shape[0]

        def row_copy(r):
            return pltpu.make_async_copy(tok_ref.at[pl.ds(r, 1), :],
                                         sorted_hbm.at[pl.ds(slot_ref[token0 + r], 1), :], sem)
        _row_copies(n, row_copy)
        pltpu.make_async_copy(tok_ref, sorted_hbm.at[pl.ds(0, n), :], sem).wait()

    @pl.when(i < n_a_tiles)
    def _():
        scatter(tok_a_ref, i * tile_a)

    @pl.when(i >= n_a_tiles)
    def _():
        scatter(tok_b_ref, n_a_tiles * tile_a + (i - n_a_tiles) * tile_b)

    @pl.when(i == pl.num_programs(0) - 1)
    def _():
        zero_s[...] = jnp.zeros_like(zero_s)
        for region in range(N_PAD_REGIONS):
            def pad_copy(k, region=region):
                return pltpu.make_async_copy(zero_s.at[pl.ds(0, 1), :],
                                             sorted_hbm.at[pl.ds(pad0_ref[region] + k, 1), :], pad_sem)

            def issue(k, carry):
                pad_copy(k).start()
                return carry

            def drain(k, carry):
                pad_copy(k).wait()
                return carry
            lax.fori_loop(0, padn_ref[region], issue, 0)
            lax.fori_loop(0, padn_ref[region], drain, 0)


def _dispatch_call(tokens_a, tokens_b, slot, pad0, padn, *, n_sorted):
    n_a, n_b = tokens_a.shape[0], tokens_b.shape[0]
    tile_a, tile_b = _copy_tile(n_a), _copy_tile(n_b)
    n_a_tiles = n_a // tile_a
    return pl.pallas_call(
        functools.partial(_dispatch_kernel, n_a_tiles=n_a_tiles),
        grid_spec=pltpu.PrefetchScalarGridSpec(
            num_scalar_prefetch=3, grid=(n_a_tiles + n_b // tile_b,),
            in_specs=[pl.BlockSpec((tile_a, TOKEN_W), lambda i, *_: (jnp.minimum(i, n_a_tiles - 1), 0)),
                      pl.BlockSpec((tile_b, TOKEN_W), lambda i, *_: (jnp.maximum(i - n_a_tiles, 0), 0))],
            out_specs=pl.BlockSpec(memory_space=pl.ANY),
            scratch_shapes=[pltpu.VMEM((SUBLANES, TOKEN_W), F32), pltpu.SemaphoreType.DMA(()),
                            pltpu.SemaphoreType.DMA(())]),
        out_shape=jax.ShapeDtypeStruct((n_sorted, TOKEN_W), F32),
        compiler_params=pltpu.CompilerParams(dimension_semantics=("arbitrary",), vmem_limit_bytes=VMEM_LIMIT),
        name="moe_dispatch",
    )(slot, pad0, padn, tokens_a, tokens_b)


def _moe_kernel(tile_group_ref, n_tiles_ref, tok_ref, wg_ref, wu_ref, wd_ref, gains_ref, y_ref,
                xn_b, he_b, y_s, st_s):
    i = pl.program_id(0)
    live = i < n_tiles_ref[0]

    def pk(r0):
        return pl.ds(r0, PACKED_ROWS)

    @pl.when(live)
    def _():
        def norm2_stats(r0):
            x1 = tok_ref[pk(r0), 0:D]
            ms = jnp.mean(x1 * x1, axis=-1, keepdims=True)
            st_s[pk(r0), :] = _on_lanes(lax.rsqrt(ms + EPS))
        _row_loop(MOE_TILE, PACKED_ROWS, norm2_stats, unroll=STATS_UNROLL)

        def norm2(r0):
            xn_b[pk(r0), :] = (tok_ref[pk(r0), 0:D] * _across(st_s[pk(r0), :]) * gains_ref[0:1, :]).astype(BF16)
        _row_loop(MOE_TILE, PACKED_ROWS, norm2, unroll=4)

        info = tok_ref[:, D:D + LANES]
        lane = lax.broadcasted_iota(jnp.int32, info.shape, 1)
        lane0 = EXPERT_LANE0 + tile_group_ref[i] * EXPERTS_PER_GROUP
        for e in range(EXPERTS_PER_GROUP):
            scale = jnp.sum(jnp.where(lane == lane0 + e, info, 0.0), axis=-1, keepdims=True)
            gate = _dot(xn_b[...], wg_ref[e])
            he = gate * _sigmoid(gate) * _dot(xn_b[...], wu_ref[e])
            he_b[:, e * D_EXPERT:(e + 1) * D_EXPERT] = (he * scale).astype(BF16)
        y_s[...] = _dot(he_b[...], wd_ref[...])

        def residual_stats(r0):
            y = tok_ref[pk(r0), 0:D] + y_s[pk(r0), :]
            y_s[pk(r0), :] = y
            ms = jnp.mean(y * y, axis=-1, keepdims=True)
            st_s[pk(r0), :] = _on_lanes(lax.rsqrt(ms + EPS))
        _row_loop(MOE_TILE, PACKED_ROWS, residual_stats, unroll=STATS_UNROLL)

        def final_norm(r0):
            y_ref[pk(r0), :] = y_s[pk(r0), :] * _across(st_s[pk(r0), :]) * gains_ref[1:2, :]
        _row_loop(MOE_TILE, PACKED_ROWS, final_norm, unroll=4)

    @pl.when(jnp.logical_not(live))
    def _():
        y_ref[...] = jnp.zeros_like(y_ref)


def _moe_call(sorted_tokens, tile_group, n_tiles, wg, wu, wd, gains):
    n_sorted = sorted_tokens.shape[0]
    assert n_sorted % MOE_TILE == 0
    hidden = EXPERTS_PER_GROUP * D_EXPERT

    def tile(i, tile_group_ref, n_tiles_ref):
        return (jnp.minimum(i, n_tiles_ref[0] - 1), 0)

    def group(i, tile_group_ref, n_tiles_ref):
        return (tile_group_ref[jnp.minimum(i, n_tiles_ref[0] - 1)], 0, 0)

    return pl.pallas_call(
        _moe_kernel,
        grid_spec=pltpu.PrefetchScalarGridSpec(
            num_scalar_prefetch=2, grid=(n_sorted // MOE_TILE,),
            in_specs=[pl.BlockSpec((MOE_TILE, TOKEN_W), tile),
                      pl.BlockSpec((EXPERTS_PER_GROUP, D, D_EXPERT), group),
                      pl.BlockSpec((EXPERTS_PER_GROUP, D, D_EXPERT), group),
                      pl.BlockSpec((None, hidden, D), group),
                      pl.BlockSpec((SUBLANES, D), lambda i, *_: (0, 0))],
            out_specs=pl.BlockSpec((MOE_TILE, D), lambda i, *_: (i, 0)),
            scratch_shapes=[pltpu.VMEM((MOE_TILE, D), BF16), pltpu.VMEM((MOE_TILE, hidden), BF16),
                            pltpu.VMEM((MOE_TILE, D), F32), pltpu.VMEM((MOE_TILE, LANES), F32)]),
        out_shape=jax.ShapeDtypeStruct((n_sorted, D), F32),
        compiler_params=pltpu.CompilerParams(dimension_semantics=("arbitrary",),
                                             vmem_limit_bytes=VMEM_LIMIT),
        name="moe_grouped",
    )(tile_group, n_tiles, sorted_tokens, wg, wu, wd, gains)


def _collect_kernel(src_ref, y_hbm, o_ref, sem):
    i = pl.program_id(0)
    tile = o_ref.shape[0]

    def row_copy(r):
        return pltpu.make_async_copy(y_hbm.at[pl.ds(src_ref[i * tile + r], 1), :],
                                     o_ref.at[pl.ds(r, 1), :], sem)
    _row_copies(tile, row_copy)
    pltpu.make_async_copy(y_hbm.at[pl.ds(0, tile), :], o_ref, sem).wait()


def _collect_call(y_sorted, src):
    n = src.shape[0]
    tile = _copy_tile(n)
    return pl.pallas_call(
        _collect_kernel,
        grid_spec=pltpu.PrefetchScalarGridSpec(
            num_scalar_prefetch=1, grid=(n // tile,),
            in_specs=[pl.BlockSpec(memory_space=pl.ANY)],
            out_specs=pl.BlockSpec((tile, D), lambda i, *_: (i, 0)),
            scratch_shapes=[pltpu.SemaphoreType.DMA(())]),
        out_shape=jax.ShapeDtypeStruct((n, D), F32),
        compiler_params=pltpu.CompilerParams(dimension_semantics=("arbitrary",), vmem_limit_bytes=VMEM_LIMIT),
        name="moe_collect",
    )(src, y_sorted)


def _routing_tables(group, n_tile_cap):
    onehot = (group[:, None] == jnp.arange(N_GROUPS, dtype=jnp.int32)[None, :]).astype(jnp.int32)
    running = jnp.cumsum(onehot, axis=0)
    rank = jnp.sum(running * onehot, axis=1) - 1
    counts = running[-1]
    tiles_per_group = (counts + MOE_TILE - 1) // MOE_TILE
    end_tile = jnp.cumsum(tiles_per_group)
    first_row = (end_tile - tiles_per_group) * MOE_TILE
    slot = jnp.sum(first_row[None, :] * onehot, axis=1) + rank
    tile_ids = jnp.arange(n_tile_cap, dtype=jnp.int32)
    tile_group = jnp.minimum(jnp.sum((tile_ids[:, None] >= end_tile[None, :]).astype(jnp.int32), axis=1),
                             N_GROUPS - 1)
    used_rows = end_tile[-1:] * MOE_TILE
    pad0 = jnp.concatenate([first_row + counts, used_rows])
    padn = jnp.concatenate([tiles_per_group * MOE_TILE - counts, n_tile_cap * MOE_TILE - used_rows])
    return (slot.astype(jnp.int32), tile_group.astype(jnp.int32), end_tile[-1:].astype(jnp.int32),
            pad0.astype(jnp.int32), padn.astype(jnp.int32))


def _block_diag_gates(w_a, w_i):
    def bd(w):
        w = w.reshape(N_GATE_BLOCKS, HEADS_PER_BLOCK, HEAD_DIM, HEAD_DIM)
        eye = jnp.eye(HEADS_PER_BLOCK, dtype=w.dtype)
        full = jnp.einsum("bhij,hg->bhigj", w, eye)
        return full.reshape(N_GATE_BLOCKS, MXU_DIM, MXU_DIM)
    return jnp.concatenate([bd(w_a), bd(w_i)], axis=-1).astype(BF16)


def _to_time_major(a, n_b):
    b, t, d = a.shape
    return a.reshape(n_b, b // n_b, t, d).transpose(0, 2, 1, 3).reshape(n_b, t * (b // n_b), d)


def _from_time_major(a, t):
    n_b, rows, d = a.shape
    bb = rows // t
    return a.reshape(n_b, t, bb, d).transpose(0, 2, 1, 3).reshape(n_b * bb, t, d)


def kernel(x_prompt, x_sample, state_rglru_h, state_rglru_conv, state_conformer_conv, meta_tokens, norm1_g, w_in, rnn_conv_w, rnn_conv_b, w_rg_a, b_rg_a, w_rg_i, b_rg_i, rg_lambda, w_rnn_proj, conf_conv_w, conf_conv_b, conf_ln_g, conf_ln_b, w_conv_proj, w_out, norm2_g, w_group, b_group, w_erouter, b_erouter, w_exp_gate, w_exp_up, w_exp_down, final_norm_g):
    assert w_in.shape[0] == 1, "single-layer trunk"
    bsz, seq, _ = x_prompt.shape
    dec_b, dec_t, _ = x_sample.shape

    vec_rows = [norm1_g[0], rnn_conv_b[0], b_rg_a[0], b_rg_i[0], rg_lambda[0], conf_conv_b[0], conf_ln_g[0],
                conf_ln_b[0], norm2_g[0]]
    vec = jnp.zeros((N_VEC_ROWS, D), F32).at[:len(vec_rows)].set(jnp.stack(vec_rows).astype(F32))
    cw4 = jnp.zeros((SUBLANES, D), F32).at[:RNN_CONV_W].set(rnn_conv_w[0])
    cw31 = jnp.zeros((4 * SUBLANES, D), F32).at[:CONF_CONV_W].set(conf_conv_w[0])
    w_router = jnp.zeros((D, LANES), F32)
    w_router = w_router.at[:, :N_GROUPS].set(w_group[0]).at[:, N_GROUPS:N_GROUPS + N_EXPERTS].set(w_erouter[0])
    wr_hi = w_router.astype(BF16)
    wr_lo = (w_router - wr_hi.astype(F32)).astype(BF16)
    wr_split = jnp.concatenate([wr_hi, wr_lo], axis=1)
    b_router = jnp.zeros((1, LANES), F32)
    b_router = b_router.at[0, :N_GROUPS].set(b_group[0]).at[0, N_GROUPS:N_GROUPS + N_EXPERTS].set(
        b_erouter[0].reshape(-1))
    wts = (vec, cw4, cw31, w_in[0].astype(BF16), _block_diag_gates(w_rg_a[0], w_rg_i[0]),
           w_rnn_proj[0].astype(BF16), w_conv_proj[0].astype(BF16), w_out[0].astype(BF16), wr_split, b_router)
    wg = w_exp_gate[0].astype(BF16)
    wu = w_exp_up[0].astype(BF16)
    wd = w_exp_down[0].astype(BF16).reshape(N_GROUPS, EXPERTS_PER_GROUP * D_EXPERT, D)
    gains = jnp.zeros((SUBLANES, D), F32).at[0].set(norm2_g[0]).at[1].set(final_norm_g)

    n_p = bsz * seq
    n_s = dec_b * dec_t

    meta_tm = jnp.broadcast_to(meta_tokens[:, None, :], (N_META, bsz, D)).reshape(1, N_META * bsz, D)
    _, h_m, c4_m, c31_m = _mixer_call(
        meta_tm, jnp.zeros((1, bsz, D), F32), jnp.zeros((1, (RNN_CONV_W - 1) * bsz, D), F32),
        jnp.zeros((1, (CONF_CONV_W - 1) * bsz, D), F32), wts, t_tile=N_META, b_blk=bsz)

    tok_p, h_p, c4_p, c31_p = _mixer_call(
        x_prompt, h_m, c4_m, c31_m, wts, t_tile=PROMPT_T_TILE, b_blk=bsz, batch_major=True)

    n_b = SAMPLE_BATCH_BLOCKS
    bb = dec_b // n_b
    tok_s, h_s, c4_s, c31_s = _mixer_call(
        _to_time_major(x_sample, n_b), state_rglru_h[0].reshape(n_b, bb, D),
        _to_time_major(state_rglru_conv[0], n_b), _to_time_major(state_conformer_conv[0], n_b), wts,
        t_tile=dec_t, b_blk=bb)

    n_tile_cap = (n_p + n_s) // MOE_TILE + N_GROUPS
    group = jnp.concatenate([tok_p[:, D], tok_s[:, D]]).astype(jnp.int32)
    slot, tile_group, n_tiles, pad0, padn = _routing_tables(group, n_tile_cap)
    sorted_tokens = _dispatch_call(tok_p, tok_s, slot, pad0, padn, n_sorted=n_tile_cap * MOE_TILE)
    y_sorted = _moe_call(sorted_tokens, tile_group, n_tiles, wg, wu, wd, gains)
    src_p = slot[:n_p].reshape(seq, bsz).T.reshape(-1)
    src_s = slot[n_p:].reshape(n_b, dec_t, bb).transpose(0, 2, 1).reshape(-1)
    y_prompt = _collect_call(y_sorted, src_p).reshape(bsz, seq, D)
    y_sample = _collect_call(y_sorted, src_s).reshape(dec_b, dec_t, D)

    return (y_prompt, y_sample,
            h_p.reshape(1, bsz, D), _from_time_major(c4_p, RNN_CONV_W - 1)[None],
            _from_time_major(c31_p, CONF_CONV_W - 1)[None],
            h_s.reshape(1, dec_b, D), _from_time_major(c4_s, RNN_CONV_W - 1)[None],
            _from_time_major(c31_s, CONF_CONV_W - 1)[None])
```

```python
import functools

import jax
import jax.numpy as jnp
from jax import lax
from jax.experimental import pallas as pl
from jax.experimental.pallas import tpu as pltpu

D = 1024
N_HEADS = 16
HEAD_DIM = D // N_HEADS
RNN_CONV_W = 4
CONF_CONV_W = 31
LRU_C = 8.0
N_GROUPS = 4
EXPERTS_PER_GROUP = 8
N_EXPERTS = N_GROUPS * EXPERTS_PER_GROUP
D_EXPERT = D // 4
EPS = 1e-6

SUBLANES = 8
PACKED_ROWS = 16
LANES = 128
N_COLS = D // LANES
MXU_DIM = 256
HEADS_PER_BLOCK = MXU_DIM // HEAD_DIM
N_GATE_BLOCKS = D // MXU_DIM
VMEM_LIMIT = 56 * 1024 * 1024
TOKEN_W = D + LANES
EXPERT_LANE0 = N_GROUPS
PROMPT_T_TILE = 32
SAMPLE_BATCH_BLOCKS = 4
MOE_TILE = 256
COPY_TILE = 2048
STATS_UNROLL = 16
N_PAD_REGIONS = N_GROUPS + 1

F32 = jnp.float32
BF16 = jnp.bfloat16

(V_NORM1, V_CONV4_B, V_BA, V_BI, V_LAMBDA, V_CONV31_B, V_LN_G, V_LN_B, V_NORM2) = range(9)
N_VEC_ROWS = 16


def _dot(a, b):
    return jnp.dot(a, b, preferred_element_type=F32)


def _sigmoid(x):
    return 0.5 * jnp.tanh(0.5 * x) + 0.5


def _on_lanes(col):
    return jnp.broadcast_to(col, (col.shape[0], LANES))


def _across(tile):
    return jnp.concatenate([tile] * N_COLS, axis=1)


def _row_loop(n_rows, chunk, body, unroll):
    n = n_rows // chunk
    assert n * chunk == n_rows

    def wrapped(s, carry):
        body(pl.multiple_of(s * chunk, chunk))
        return carry
    lax.fori_loop(0, n, wrapped, 0, unroll=min(unroll, n))


def _route_info(lg):
    neg = float("-inf")
    lane = lax.broadcasted_iota(jnp.int32, lg.shape, 1).astype(F32)
    no_lane = float(LANES)

    def first_lane_of_max(vals, vmax):
        return jnp.min(jnp.where(vals == vmax, lane, no_lane), axis=-1, keepdims=True)

    gmask = lane < N_GROUPS
    gl = jnp.where(gmask, lg, neg)
    gmax = jnp.max(gl, axis=-1, keepdims=True)
    gidx = first_lane_of_max(gl, gmax)
    gsum = jnp.sum(jnp.where(gmask, jnp.exp(gl - gmax), 0.0), axis=-1, keepdims=True)
    g_gate = 1.0 / gsum
    first = EXPERT_LANE0 + gidx * EXPERTS_PER_GROUP
    emask = (lane >= first) & (lane < first + EXPERTS_PER_GROUP)
    el = jnp.where(emask, lg, neg)
    v1 = jnp.max(el, axis=-1, keepdims=True)
    i1 = first_lane_of_max(el, v1)
    el2 = jnp.where(lane == i1, neg, el)
    v2 = jnp.max(el2, axis=-1, keepdims=True)
    i2 = first_lane_of_max(el2, v2)
    e2 = jnp.exp(v2 - v1)
    w1 = 1.0 / (1.0 + e2)
    w2 = e2 / (1.0 + e2)
    comb = jnp.where(lane == i1, w1 * g_gate, 0.0) + jnp.where(lane == i2, w2 * g_gate, 0.0)
    return jnp.where(lane == 0.0, gidx, comb)


def _mixer_kernel(x_ref, prefix_ref, h0_ref, c4_ref, c31_ref, vec_ref, cw4_ref, cw31_ref,
                  w_in_ref, w_gate_ref, w_rnn_ref, w_conv_ref, w_out_ref, wr_ref, br_ref,
                  x1c_ref, h_out_ref, c4_out_ref, c31_out_ref,
                  xn_b, ux_s, uy_s, cv_s, cg_s, grnn_s, gconv_s, glu_s, xc_s, xc_b, ga_s, gi_s, hy_s, cn_b, h_s,
                  x_s, st_s, st2_s, *, t_tile, b_blk, batch_major, prefix_t):
    i = pl.program_id(1)
    rows = t_tile * b_blk
    p4 = (RNN_CONV_W - 1) * b_blk
    p31 = (CONF_CONV_W - 1) * b_blk

    @pl.when(i == 0)
    def _():
        h_s[...] = h0_ref[...]
        ux_s[0:p4, :] = c4_ref[...]
        glu_s[0:p31, :] = c31_ref[...]

    def pk(r0):
        return pl.ds(r0, PACKED_ROWS)

    def vec(row):
        return vec_ref[row:row + 1, :]

    def x_rows(ref, r0):
        if not batch_major:
            return ref[pk(r0), :]
        t = r0 // b_blk
        return jnp.concatenate([ref[:, t + k, :] for k in range(PACKED_ROWS // b_blk)], axis=0)

    def norm1_stats_of(ref):
        def norm1_stats(r0):
            x = x_rows(ref, r0)
            x_s[pk(r0), :] = x
            ms = jnp.mean(x * x, axis=-1, keepdims=True)
            st_s[pk(r0), :] = _on_lanes(lax.rsqrt(ms + EPS))
        _row_loop(rows, PACKED_ROWS, norm1_stats, unroll=STATS_UNROLL)

    if prefix_t:
        pl.when(i == 0)(lambda: norm1_stats_of(prefix_ref))
        pl.when(i > 0)(lambda: norm1_stats_of(x_ref))
    else:
        norm1_stats_of(x_ref)

    def norm1_apply(r0):
        xn_b[pk(r0), :] = (x_s[pk(r0), :] * _across(st_s[pk(r0), :]) * vec(V_NORM1)).astype(BF16)
    _row_loop(rows, PACKED_ROWS, norm1_apply, unroll=4)

    def in_proj(group):
        return _dot(xn_b[...], w_in_ref[:, group * D:(group + 1) * D])
    ux_s[p4:p4 + rows, :] = in_proj(0)
    uy_s[...] = in_proj(1)
    cv_s[...] = in_proj(2)
    cg_s[...] = in_proj(3)
    grnn_s[...] = in_proj(4)
    gconv_s[...] = in_proj(5)

    def conv4_glu(r0):
        acc = ux_s[pk(r0), :] * cw4_ref[0:1, :]
        for k in range(1, RNN_CONV_W):
            acc = acc + ux_s[pk(r0 + k * b_blk), :] * cw4_ref[k:k + 1, :]
        xc = acc + vec(V_CONV4_B)
        xc_s[pk(r0), :] = xc
        xc_b[pk(r0), :] = xc.astype(BF16)
        glu_s[pk(p31 + r0), :] = cv_s[pk(r0), :] * _sigmoid(cg_s[pk(r0), :])
    _row_loop(rows, PACKED_ROWS, conv4_glu, unroll=2)

    strip_t = min(SUBLANES, t_tile)
    n_strip_t = t_tile // strip_t
    assert n_strip_t * strip_t == t_tile

    def conv31_strip(q, carry):
        jb = q // n_strip_t
        tc = q - jb * n_strip_t
        r0 = pl.multiple_of(tc * (strip_t * b_blk) + jb * SUBLANES, SUBLANES)
        for col in range(N_COLS):
            cs = slice(col * LANES, (col + 1) * LANES)
            xs = [glu_s[pl.ds(r0 + j * b_blk, SUBLANES), cs] for j in range(strip_t + CONF_CONV_W - 1)]
            accs = [None] * strip_t
            for k in range(CONF_CONV_W):
                w = cw31_ref[k:k + 1, cs]
                for o in range(strip_t):
                    term = xs[o + k] * w
                    accs[o] = term if k == 0 else accs[o] + term
            bias = vec_ref[V_CONV31_B:V_CONV31_B + 1, cs]
            for o in range(strip_t):
                cv_s[pl.ds(r0 + o * b_blk, SUBLANES), cs] = accs[o] + bias
        return carry
    lax.fori_loop(0, (b_blk // SUBLANES) * n_strip_t, conv31_strip, 0)

    for blk in range(N_GATE_BLOCKS):
        cols = slice(blk * MXU_DIM, (blk + 1) * MXU_DIM)
        g = _dot(xc_b[:, cols], w_gate_ref[blk])
        ga_s[:, cols] = g[:, :MXU_DIM]
        gi_s[:, cols] = g[:, MXU_DIM:]

    c_lam = -LRU_C * jax.nn.softplus(-vec(V_LAMBDA))

    def lru_coeffs(r0):
        r = _sigmoid(ga_s[pk(r0), :] + vec(V_BA))
        ig = _sigmoid(gi_s[pk(r0), :] + vec(V_BI))
        log_a = c_lam * r
        a = jnp.exp(log_a)
        mult = jnp.sqrt(-jnp.tanh(log_a) * (a * a + 1.0))
        ga_s[pk(r0), :] = a
        gi_s[pk(r0), :] = mult * (ig * xc_s[pk(r0), :])
        uy_s[pk(r0), :] = jax.nn.gelu(uy_s[pk(r0), :])
    _row_loop(rows, PACKED_ROWS, lru_coeffs, unroll=2)

    if prefix_t:
        @pl.when(i == 0)
        def _():
            lead = (t_tile - prefix_t) * b_blk
            gi_s[0:lead, :] = jnp.zeros((lead, D), F32)

    def scan_batch_slab(jb, carry):
        b0 = pl.multiple_of(jb * SUBLANES, SUBLANES)

        def step(t, h):
            sl = pl.ds(pl.multiple_of(t * b_blk + b0, SUBLANES), SUBLANES)
            h = ga_s[sl, :] * h + gi_s[sl, :]
            hy_s[sl, :] = h * uy_s[sl, :]
            return h

        h_s[pl.ds(b0, SUBLANES), :] = lax.fori_loop(0, t_tile, step, h_s[pl.ds(b0, SUBLANES), :],
                                                    unroll=min(8, t_tile))
        return carry
    lax.fori_loop(0, b_blk // SUBLANES, scan_batch_slab, 0)

    def ln_stats(r0):
        cc = cv_s[pk(r0), :]
        mu = jnp.mean(cc, axis=-1, keepdims=True)
        cen = cc - mu
        var = jnp.mean(cen * cen, axis=-1, keepdims=True)
        st_s[pk(r0), :] = _on_lanes(mu)
        st2_s[pk(r0), :] = _on_lanes(lax.rsqrt(var + EPS))
    _row_loop(rows, PACKED_ROWS, ln_stats, unroll=STATS_UNROLL)

    def ln_silu(r0):
        cen = cv_s[pk(r0), :] - _across(st_s[pk(r0), :])
        y = cen * _across(st2_s[pk(r0), :]) * vec(V_LN_G) + vec(V_LN_B)
        cn_b[pk(r0), :] = (y * _sigmoid(y)).astype(BF16)
    _row_loop(rows, PACKED_ROWS, ln_silu, unroll=4)

    ga_s[...] = _dot(hy_s[...].astype(BF16), w_rnn_ref[...])
    gi_s[...] = _dot(cn_b[...], w_conv_ref[...])

    def merge(r0):
        xn_b[pk(r0), :] = (_sigmoid(grnn_s[pk(r0), :]) * ga_s[pk(r0), :]
                           + _sigmoid(gconv_s[pk(r0), :]) * gi_s[pk(r0), :]).astype(BF16)
    _row_loop(rows, PACKED_ROWS, merge, unroll=2)
    cg_s[...] = _dot(xn_b[...], w_out_ref[...])

    def residual_stats(r0):
        x1 = x_s[pk(r0), :] + cg_s[pk(r0), :]
        x1c_ref[pk(r0), 0:D] = x1
        ms = jnp.mean(x1 * x1, axis=-1, keepdims=True)
        st_s[pk(r0), :] = _on_lanes(lax.rsqrt(ms + EPS))
    _row_loop(rows, PACKED_ROWS, residual_stats, unroll=STATS_UNROLL)

    def norm2_split(r0):
        xn2 = x1c_ref[pk(r0), 0:D] * _across(st_s[pk(r0), :]) * vec(V_NORM2)
        hi = xn2.astype(BF16)
        cn_b[pk(r0), :] = hi
        xc_b[pk(r0), :] = (xn2 - hi.astype(F32)).astype(BF16)
    _row_loop(rows, PACKED_ROWS, norm2_split, unroll=4)

    hi_prod = _dot(cn_b[...], wr_ref[...])
    logits = hi_prod[:, :LANES] + hi_prod[:, LANES:] + _dot(xc_b[...], wr_ref[:, :LANES]) + br_ref[...]
    x1c_ref[:, D:D + LANES] = _route_info(logits)

    for k in range(RNN_CONV_W - 1):
        ux_s[k * b_blk:(k + 1) * b_blk, :] = ux_s[(t_tile + k) * b_blk:(t_tile + k + 1) * b_blk, :]
    for k in range(CONF_CONV_W - 1):
        glu_s[k * b_blk:(k + 1) * b_blk, :] = glu_s[(t_tile + k) * b_blk:(t_tile + k + 1) * b_blk, :]

    @pl.when(i == pl.num_programs(1) - 1)
    def _():
        h_out_ref[...] = h_s[...]
        c4_out_ref[...] = ux_s[0:p4, :]
        c31_out_ref[...] = glu_s[0:p31, :]


def _mixer_call(x, h0, c4, c31, wts, *, t_tile, b_blk, batch_major=False, prefix_tokens=None):
    if batch_major:
        assert x.shape[0] == b_blk and PACKED_ROWS % b_blk == 0
        n_b, total_rows = 1, x.shape[0] * x.shape[1]
        tile_shape = (b_blk, t_tile, D)
    else:
        n_b, total_rows, _ = x.shape
        tile_shape = (t_tile * b_blk, D)
    rows = t_tile * b_blk
    n_t = total_rows // rows
    assert n_t * rows == total_rows and b_blk % SUBLANES == 0 and rows % PACKED_ROWS == 0
    assert t_tile >= RNN_CONV_W - 1
    p4 = (RNN_CONV_W - 1) * b_blk
    p31 = (CONF_CONV_W - 1) * b_blk
    if prefix_tokens is None:
        prefix_t, n_pre = 0, 0
        prefix = jnp.zeros(tile_shape, F32)
    else:
        assert batch_major and n_b == 1
        prefix_t, n_pre = prefix_tokens.shape[0], 1
        lead = jnp.zeros((b_blk, t_tile - prefix_t, D), F32)
        prefix = jnp.concatenate([lead, jnp.broadcast_to(prefix_tokens[None], (b_blk, prefix_t, D))], axis=1)

    def x_step(i):
        return jnp.maximum(i - n_pre, 0)

    def tile_spec(width):
        return pl.BlockSpec((None, rows, width), lambda j, i: (j, x_step(i), 0))

    def state_spec(n_rows):
        return pl.BlockSpec((None, n_rows, D), lambda j, i: (j, 0, 0))

    def const_spec(arr):
        nd = arr.ndim
        return pl.BlockSpec(arr.shape, lambda j, i: (0,) * nd, pipeline_mode=pl.Buffered(1))

    plane = pltpu.VMEM((rows, D), F32)
    plane_b = pltpu.VMEM((rows, D), BF16)
    return pl.pallas_call(
        functools.partial(_mixer_kernel, t_tile=t_tile, b_blk=b_blk, batch_major=batch_major, prefix_t=prefix_t),
        grid=(n_b, n_t + n_pre),
        in_specs=[pl.BlockSpec((b_blk, t_tile, D), lambda j, i: (0, x_step(i), 0)) if batch_major
                  else tile_spec(D),
                  const_spec(prefix), state_spec(b_blk), state_spec(p4), state_spec(p31)]
                 + [const_spec(w) for w in wts],
        out_specs=[pl.BlockSpec((rows, TOKEN_W), lambda j, i: (j * n_t + x_step(i), 0)),
                   state_spec(b_blk), state_spec(p4), state_spec(p31)],
        out_shape=[jax.ShapeDtypeStruct((n_b * total_rows, TOKEN_W), F32),
                   jax.ShapeDtypeStruct((n_b, b_blk, D), F32),
                   jax.ShapeDtypeStruct((n_b, p4, D), F32),
                   jax.ShapeDtypeStruct((n_b, p31, D), F32)],
        scratch_shapes=[plane_b, pltpu.VMEM((p4 + rows, D), F32), plane, plane, plane, plane, plane,
                        pltpu.VMEM((p31 + rows, D), F32), plane, plane_b, plane, plane, plane, plane_b,
                        pltpu.VMEM((b_blk, D), F32), plane, pltpu.VMEM((rows, LANES), F32),
                        pltpu.VMEM((rows, LANES), F32)],
        compiler_params=pltpu.CompilerParams(dimension_semantics=("arbitrary", "arbitrary"),
                                             vmem_limit_bytes=VMEM_LIMIT),
        name="mixer",
    )(x, prefix, h0, c4, c31, *wts)


def _row_copies(n, make_copy):
    def issue(r, carry):
        make_copy(r).start()
        return carry
    lax.fori_loop(0, n, issue, 0, unroll=8)


def _copy_tile(n_rows):
    tile = COPY_TILE
    while n_rows % tile:
        tile //= 2
    return tile


def _dispatch_kernel(slot_ref, pad0_ref, padn_ref, tok_a_ref, tok_b_ref, sorted_hbm, zero_s, sem, pad_sem,
                     *, n_a_tiles):
    i = pl.program_id(0)
    tile_a, tile_b = tok_a_ref.shape[0], tok_b_ref.shape[0]

    def scatter(tok_ref, token0):
        n = tok_ref.shape[0]

        def row_copy(r):
            return pltpu.make_async_copy(tok_ref.at[pl.ds(r, 1), :],
                                         sorted_hbm.at[pl.ds(slot_ref[token0 + r], 1), :], sem)
        _row_copies(n, row_copy)
        pltpu.make_async_copy(tok_ref, sorted_hbm.at[pl.ds(0, n), :], sem).wait()

    @pl.when(i < n_a_tiles)
    def _():
        scatter(tok_a_ref, i * tile_a)

    @pl.when(i >= n_a_tiles)
    def _():
        scatter(tok_b_ref, n_a_tiles * tile_a + (i - n_a_tiles) * tile_b)

    @pl.when(i == pl.num_programs(0) - 1)
    def _():
        zero_s[...] = jnp.zeros_like(zero_s)
        for region in range(N_PAD_REGIONS):
            def pad_copy(k, region=region):
                return pltpu.make_async_copy(zero_s.at[pl.ds(0, 1), :],
                                             sorted_hbm.at[pl.ds(pad0_ref[region] + k, 1), :], pad_sem)

            def issue(k, carry):
                pad_copy(k).start()
                return carry

            def drain(k, carry):
                pad_copy(k).wait()
                return carry
            lax.fori_loop(0, padn_ref[region], issue, 0)
            lax.fori_loop(0, padn_ref[region], drain, 0)


def _dispatch_call(tokens_a, tokens_b, slot, pad0, padn, *, n_sorted):
    n_a, n_b = tokens_a.shape[0], tokens_b.shape[0]
    tile_a, tile_b = _copy_tile(n_a), _copy_tile(n_b)
    n_a_tiles = n_a // tile_a
    return pl.pallas_call(
        functools.partial(_dispatch_kernel, n_a_tiles=n_a_tiles),
        grid_spec=pltpu.PrefetchScalarGridSpec(
            num_scalar_prefetch=3, grid=(n_a_tiles + n_b // tile_b,),
            in_specs=[pl.BlockSpec((tile_a, TOKEN_W), lambda i, *_: (jnp.minimum(i, n_a_tiles - 1), 0)),
                      pl.BlockSpec((tile_b, TOKEN_W), lambda i, *_: (jnp.maximum(i - n_a_tiles, 0), 0))],
            out_specs=pl.BlockSpec(memory_space=pl.ANY),
            scratch_shapes=[pltpu.VMEM((SUBLANES, TOKEN_W), F32), pltpu.SemaphoreType.DMA(()),
                            pltpu.SemaphoreType.DMA(())]),
        out_shape=jax.ShapeDtypeStruct((n_sorted, TOKEN_W), F32),
        compiler_params=pltpu.CompilerParams(dimension_semantics=("arbitrary",), vmem_limit_bytes=VMEM_LIMIT),
        name="moe_dispatch",
    )(slot, pad0, padn, tokens_a, tokens_b)


def _moe_kernel(tile_group_ref, n_tiles_ref, tok_ref, wg_ref, wu_ref, wd_ref, gains_ref, y_ref,
                xn_b, he_b, y_s, st_s):
    i = pl.program_id(0)
    live = i < n_tiles_ref[0]

    def pk(r0):
        return pl.ds(r0, PACKED_ROWS)

    @pl.when(live)
    def _():
        def norm2_stats(r0):
            x1 = tok_ref[pk(r0), 0:D]
            ms = jnp.mean(x1 * x1, axis=-1, keepdims=True)
            st_s[pk(r0), :] = _on_lanes(lax.rsqrt(ms + EPS))
        _row_loop(MOE_TILE, PACKED_ROWS, norm2_stats, unroll=STATS_UNROLL)

        def norm2(r0):
            xn_b[pk(r0), :] = (tok_ref[pk(r0), 0:D] * _across(st_s[pk(r0), :]) * gains_ref[0:1, :]).astype(BF16)
        _row_loop(MOE_TILE, PACKED_ROWS, norm2, unroll=4)

        info = tok_ref[:, D:D + LANES]
        lane = lax.broadcasted_iota(jnp.int32, info.shape, 1)
        lane0 = EXPERT_LANE0 + tile_group_ref[i] * EXPERTS_PER_GROUP
        for e in range(EXPERTS_PER_GROUP):
            scale = jnp.sum(jnp.where(lane == lane0 + e, info, 0.0), axis=-1, keepdims=True)
            gate = _dot(xn_b[...], wg_ref[e])
            he = gate * _sigmoid(gate) * _dot(xn_b[...], wu_ref[e])
            he_b[:, e * D_EXPERT:(e + 1) * D_EXPERT] = (he * scale).astype(BF16)
        y_s[...] = _dot(he_b[...], wd_ref[...])

        def residual_stats(r0):
            y = tok_ref[pk(r0), 0:D] + y_s[pk(r0), :]
            y_s[pk(r0), :] = y
            ms = jnp.mean(y * y, axis=-1, keepdims=True)
            st_s[pk(r0), :] = _on_lanes(lax.rsqrt(ms + EPS))
        _row_loop(MOE_TILE, PACKED_ROWS, residual_stats, unroll=STATS_UNROLL)

        def final_norm(r0):
            y_ref[pk(r0), :] = y_s[pk(r0), :] * _across(st_s[pk(r0), :]) * gains_ref[1:2, :]
        _row_loop(MOE_TILE, PACKED_ROWS, final_norm, unroll=4)

    @pl.when(jnp.logical_not(live))
    def _():
        y_ref[...] = jnp.zeros_like(y_ref)


def _moe_call(sorted_tokens, tile_group, n_tiles, wg, wu, wd, gains):
    n_sorted = sorted_tokens.shape[0]
    assert n_sorted % MOE_TILE == 0
    hidden = EXPERTS_PER_GROUP * D_EXPERT

    def tile(i, tile_group_ref, n_tiles_ref):
        return (jnp.minimum(i, n_tiles_ref[0] - 1), 0)

    def group(i, tile_group_ref, n_tiles_ref):
        return (tile_group_ref[jnp.minimum(i, n_tiles_ref[0] - 1)], 0, 0)

    return pl.pallas_call(
        _moe_kernel,
        grid_spec=pltpu.PrefetchScalarGridSpec(
            num_scalar_prefetch=2, grid=(n_sorted // MOE_TILE,),
            in_specs=[pl.BlockSpec((MOE_TILE, TOKEN_W), tile),
                      pl.BlockSpec((EXPERTS_PER_GROUP, D, D_EXPERT), group),
                      pl.BlockSpec((EXPERTS_PER_GROUP, D, D_EXPERT), group),
                      pl.BlockSpec((None, hidden, D), group),
                      pl.BlockSpec((SUBLANES, D), lambda i, *_: (0, 0))],
            out_specs=pl.BlockSpec((MOE_TILE, D), lambda i, *_: (i, 0)),
            scratch_shapes=[pltpu.VMEM((MOE_TILE, D), BF16), pltpu.VMEM((MOE_TILE, hidden), BF16),
                            pltpu.VMEM((MOE_TILE, D), F32), pltpu.VMEM((MOE_TILE, LANES), F32)]),
        out_shape=jax.ShapeDtypeStruct((n_sorted, D), F32),
        compiler_params=pltpu.CompilerParams(dimension_semantics=("arbitrary",),
                                             vmem_limit_bytes=VMEM_LIMIT),
        name="moe_grouped",
    )(tile_group, n_tiles, sorted_tokens, wg, wu, wd, gains)


def _collect_kernel(src_ref, y_hbm, o_ref, sem):
    i = pl.program_id(0)
    tile = o_ref.shape[0]

    def row_copy(r):
        return pltpu.make_async_copy(y_hbm.at[pl.ds(src_ref[i * tile + r], 1), :],
                                     o_ref.at[pl.ds(r, 1), :], sem)
    _row_copies(tile, row_copy)
    pltpu.make_async_copy(y_hbm.at[pl.ds(0, tile), :], o_ref, sem).wait()


def _collect_call(y_sorted, src):
    n = src.shape[0]
    tile = _copy_tile(n)
    return pl.pallas_call(
        _collect_kernel,
        grid_spec=pltpu.PrefetchScalarGridSpec(
            num_scalar_prefetch=1, grid=(n // tile,),
            in_specs=[pl.BlockSpec(memory_space=pl.ANY)],
            out_specs=pl.BlockSpec((tile, D), lambda i, *_: (i, 0)),
            scratch_shapes=[pltpu.SemaphoreType.DMA(())]),
        out_shape=jax.ShapeDtypeStruct((n, D), F32),
        compiler_params=pltpu.CompilerParams(dimension_semantics=("arbitrary",), vmem_limit_bytes=VMEM_LIMIT),
        name="moe_collect",
    )(src, y_sorted)


def _routing_tables(group, n_tile_cap):
    onehot = (group[:, None] == jnp.arange(N_GROUPS, dtype=jnp.int32)[None, :]).astype(jnp.int32)
    running = jnp.cumsum(onehot, axis=0)
    rank = jnp.sum(running * onehot, axis=1) - 1
    counts = running[-1]
    tiles_per_group = (counts + MOE_TILE - 1) // MOE_TILE
    end_tile = jnp.cumsum(tiles_per_group)
    first_row = (end_tile - tiles_per_group) * MOE_TILE
    slot = jnp.sum(first_row[None, :] * onehot, axis=1) + rank
    tile_ids = jnp.arange(n_tile_cap, dtype=jnp.int32)
    tile_group = jnp.minimum(jnp.sum((tile_ids[:, None] >= end_tile[None, :]).astype(jnp.int32), axis=1),
                             N_GROUPS - 1)
    used_rows = end_tile[-1:] * MOE_TILE
    pad0 = jnp.concatenate([first_row + counts, used_rows])
    padn = jnp.concatenate([tiles_per_group * MOE_TILE - counts, n_tile_cap * MOE_TILE - used_rows])
    return (slot.astype(jnp.int32), tile_group.astype(jnp.int32), end_tile[-1:].astype(jnp.int32),
            pad0.astype(jnp.int32), padn.astype(jnp.int32))


def _block_diag_gates(w_a, w_i):
    def bd(w):
        w = w.reshape(N_GATE_BLOCKS, HEADS_PER_BLOCK, HEAD_DIM, HEAD_DIM)
        eye = jnp.eye(HEADS_PER_BLOCK, dtype=w.dtype)
        full = jnp.einsum("bhij,hg->bhigj", w, eye)
        return full.reshape(N_GATE_BLOCKS, MXU_DIM, MXU_DIM)
    return jnp.concatenate([bd(w_a), bd(w_i)], axis=-1).astype(BF16)


def _to_time_major(a, n_b):
    b, t, d = a.shape
    return a.reshape(n_b, b // n_b, t, d).transpose(0, 2, 1, 3).reshape(n_b, t * (b // n_b), d)


def _from_time_major(a, t):
    n_b, rows, d = a.shape
    bb = rows // t
    return a.reshape(n_b, t, bb, d).transpose(0, 2, 1, 3).reshape(n_b * bb, t, d)


def kernel(x_prompt, x_sample, state_rglru_h, state_rglru_conv, state_conformer_conv, meta_tokens, norm1_g, w_in, rnn_conv_w, rnn_conv_b, w_rg_a, b_rg_a, w_rg_i, b_rg_i, rg_lambda, w_rnn_proj, conf_conv_w, conf_conv_b, conf_ln_g, conf_ln_b, w_conv_proj, w_out, norm2_g, w_group, b_group, w_erouter, b_erouter, w_exp_gate, w_exp_up, w_exp_down, final_norm_g):
    assert w_in.shape[0] == 1, "single-layer trunk"
    bsz, seq, _ = x_prompt.shape
    dec_b, dec_t, _ = x_sample.shape

    vec_rows = [norm1_g[0], rnn_conv_b[0], b_rg_a[0], b_rg_i[0], rg_lambda[0], conf_conv_b[0], conf_ln_g[0],
                conf_ln_b[0], norm2_g[0]]
    vec = jnp.zeros((N_VEC_ROWS, D), F32).at[:len(vec_rows)].set(jnp.stack(vec_rows).astype(F32))
    cw4 = jnp.zeros((SUBLANES, D), F32).at[:RNN_CONV_W].set(rnn_conv_w[0])
    cw31 = jnp.zeros((4 * SUBLANES, D), F32).at[:CONF_CONV_W].set(conf_conv_w[0])
    w_router = jnp.zeros((D, LANES), F32)
    w_router = w_router.at[:, :N_GROUPS].set(w_group[0]).at[:, N_GROUPS:N_GROUPS + N_EXPERTS].set(w_erouter[0])
    wr_hi = w_router.astype(BF16)
    wr_lo = (w_router - wr_hi.astype(F32)).astype(BF16)
    wr_split = jnp.concatenate([wr_hi, wr_lo], axis=1)
    b_router = jnp.zeros((1, LANES), F32)
    b_router = b_router.at[0, :N_GROUPS].set(b_group[0]).at[0, N_GROUPS:N_GROUPS + N_EXPERTS].set(
        b_erouter[0].reshape(-1))
    wts = (vec, cw4, cw31, w_in[0].astype(BF16), _block_diag_gates(w_rg_a[0], w_rg_i[0]),
           w_rnn_proj[0].astype(BF16), w_conv_proj[0].astype(BF16), w_out[0].astype(BF16), wr_split, b_router)
    wg = w_exp_gate[0].astype(BF16)
    wu = w_exp_up[0].astype(BF16)
    wd = w_exp_down[0].astype(BF16).reshape(N_GROUPS, EXPERTS_PER_GROUP * D_EXPERT, D)
    gains = jnp.zeros((SUBLANES, D), F32).at[0].set(norm2_g[0]).at[1].set(final_norm_g)

    n_p = bsz * seq
    n_s = dec_b * dec_t

    tok_p, h_p, c4_p, c31_p = _mixer_call(
        x_prompt, jnp.zeros((1, bsz, D), F32), jnp.zeros((1, (RNN_CONV_W - 1) * bsz, D), F32),
        jnp.zeros((1, (CONF_CONV_W - 1) * bsz, D), F32), wts, t_tile=PROMPT_T_TILE, b_blk=bsz, batch_major=True,
        prefix_tokens=meta_tokens.astype(F32))

    n_b = SAMPLE_BATCH_BLOCKS
    bb = dec_b // n_b
    tok_s, h_s, c4_s, c31_s = _mixer_call(
        _to_time_major(x_sample, n_b), state_rglru_h[0].reshape(n_b, bb, D),
        _to_time_major(state_rglru_conv[0], n_b), _to_time_major(state_conformer_conv[0], n_b), wts,
        t_tile=dec_t, b_blk=bb)

    n_tile_cap = (n_p + n_s) // MOE_TILE + N_GROUPS
    group = jnp.concatenate([tok_p[:, D], tok_s[:, D]]).astype(jnp.int32)
    slot, tile_group, n_tiles, pad0, padn = _routing_tables(group, n_tile_cap)
    sorted_tokens = _dispatch_call(tok_p, tok_s, slot, pad0, padn, n_sorted=n_tile_cap * MOE_TILE)
    y_sorted = _moe_call(sorted_tokens, tile_group, n_tiles, wg, wu, wd, gains)
    src_p = slot[:n_p].reshape(seq, bsz).T.reshape(-1)
    src_s = slot[n_p:].reshape(n_b, dec_t, bb).transpose(0, 2, 1).reshape(-1)
    y_prompt = _collect_call(y_sorted, src_p).reshape(bsz, seq, D)
    y_sample = _collect_call(y_sorted, src_s).reshape(dec_b, dec_t, D)

    return (y_prompt, y_sample,
            h_p.reshape(1, bsz, D), _from_time_major(c4_p, RNN_CONV_W - 1)[None],
            _from_time_major(c31_p, CONF_CONV_W - 1)[None],
            h_s.reshape(1, dec_b, D), _from_time_major(c4_s, RNN_CONV_W - 1)[None],
            _from_time_major(c31_s, CONF_CONV_W - 1)[None])
```

```python
import functools

import jax
import jax.numpy as jnp
from jax import lax
from jax.experimental import pallas as pl
from jax.experimental.pallas import tpu as pltpu

D = 1024
N_HEADS = 16
HEAD_DIM = D // N_HEADS
RNN_CONV_W = 4
CONF_CONV_W = 31
LRU_C = 8.0
N_GROUPS = 4
EXPERTS_PER_GROUP = 8
N_EXPERTS = N_GROUPS * EXPERTS_PER_GROUP
D_EXPERT = D // 4
EPS = 1e-6

SUBLANES = 8
PACKED_ROWS = 16
LANES = 128
N_COLS = D // LANES
MXU_DIM = 256
HEADS_PER_BLOCK = MXU_DIM // HEAD_DIM
N_GATE_BLOCKS = D // MXU_DIM
VMEM_LIMIT = 56 * 1024 * 1024
TOKEN_W = D + LANES
EXPERT_LANE0 = N_GROUPS
PROMPT_T_TILE = 32
SAMPLE_BATCH_BLOCKS = 4
MOE_TILE = 256
COPY_TILE = 2048
STATS_UNROLL = 16
N_PAD_REGIONS = N_GROUPS + 1

F32 = jnp.float32
BF16 = jnp.bfloat16

(V_NORM1, V_CONV4_B, V_BA, V_BI, V_LAMBDA, V_CONV31_B, V_LN_G, V_LN_B, V_NORM2) = range(9)
N_VEC_ROWS = 16


def _dot(a, b):
    return jnp.dot(a, b, preferred_element_type=F32)


def _sigmoid(x):
    return 0.5 * jnp.tanh(0.5 * x) + 0.5


def _on_lanes(col):
    return jnp.broadcast_to(col, (col.shape[0], LANES))


def _across(tile):
    return jnp.concatenate([tile] * N_COLS, axis=1)


def _row_loop(n_rows, chunk, body, unroll):
    n = n_rows // chunk
    assert n * chunk == n_rows

    def wrapped(s, carry):
        body(pl.multiple_of(s * chunk, chunk))
        return carry
    lax.fori_loop(0, n, wrapped, 0, unroll=min(unroll, n))


def _route_info(lg):
    neg = float("-inf")
    lane = lax.broadcasted_iota(jnp.int32, lg.shape, 1).astype(F32)
    no_lane = float(LANES)

    def first_lane_of_max(vals, vmax):
        return jnp.min(jnp.where(vals == vmax, lane, no_lane), axis=-1, keepdims=True)

    gmask = lane < N_GROUPS
    gl = jnp.where(gmask, lg, neg)
    gmax = jnp.max(gl, axis=-1, keepdims=True)
    gidx = first_lane_of_max(gl, gmax)
    gsum = jnp.sum(jnp.where(gmask, jnp.exp(gl - gmax), 0.0), axis=-1, keepdims=True)
    g_gate = 1.0 / gsum
    first = EXPERT_LANE0 + gidx * EXPERTS_PER_GROUP
    emask = (lane >= first) & (lane < first + EXPERTS_PER_GROUP)
    el = jnp.where(emask, lg, neg)
    v1 = jnp.max(el, axis=-1, keepdims=True)
    i1 = first_lane_of_max(el, v1)
    el2 = jnp.where(lane == i1, neg, el)
    v2 = jnp.max(el2, axis=-1, keepdims=True)
    i2 = first_lane_of_max(el2, v2)
    e2 = jnp.exp(v2 - v1)
    w1 = 1.0 / (1.0 + e2)
    w2 = e2 / (1.0 + e2)
    comb = jnp.where(lane == i1, w1 * g_gate, 0.0) + jnp.where(lane == i2, w2 * g_gate, 0.0)
    return jnp.where(lane == 0.0, gidx, comb)


def _mixer_kernel(x_ref, prefix_ref, h0_ref, c4_ref, c31_ref, vec_ref, cw4_ref, cw31_ref,
                  w_in_ref, w_gate_ref, w_rnn_ref, w_conv_ref, w_out_ref, wr_ref, br_ref,
                  x1c_ref, h_out_ref, c4_out_ref, c31_out_ref,
                  xn_b, ux_s, uy_s, cv_s, cg_s, grnn_s, gconv_s, glu_s, xc_s, xc_b, ga_s, gi_s, hy_s, cn_b, h_s,
                  x_s, st_s, st2_s, *, t_tile, b_blk, batch_major, prefix_t):
    i = pl.program_id(1)
    rows = t_tile * b_blk
    p4 = (RNN_CONV_W - 1) * b_blk
    p31 = (CONF_CONV_W - 1) * b_blk

    @pl.when(i == 0)
    def _():
        h_s[...] = h0_ref[...]
        ux_s[0:p4, :] = c4_ref[...]
        glu_s[0:p31, :] = c31_ref[...]

    def pk(r0):
        return pl.ds(r0, PACKED_ROWS)

    def vec(row):
        return vec_ref[row:row + 1, :]

    def for_token_tiles(fn):
        if prefix_t:
            @pl.when(i > 0)
            def _():
                fn()
        else:
            fn()

    def x_rows(ref, r0):
        if not batch_major:
            return ref[pk(r0), :]
        t = r0 // b_blk
        return jnp.concatenate([ref[:, t + k, :] for k in range(PACKED_ROWS // b_blk)], axis=0)

    def norm1_stats_of(ref):
        def norm1_stats(r0):
            x = x_rows(ref, r0)
            x_s[pk(r0), :] = x
            ms = jnp.mean(x * x, axis=-1, keepdims=True)
            st_s[pk(r0), :] = _on_lanes(lax.rsqrt(ms + EPS))
        _row_loop(rows, PACKED_ROWS, norm1_stats, unroll=STATS_UNROLL)

    if prefix_t:
        pl.when(i == 0)(lambda: norm1_stats_of(prefix_ref))
        pl.when(i > 0)(lambda: norm1_stats_of(x_ref))
    else:
        norm1_stats_of(x_ref)

    def norm1_apply(r0):
        xn_b[pk(r0), :] = (x_s[pk(r0), :] * _across(st_s[pk(r0), :]) * vec(V_NORM1)).astype(BF16)
    _row_loop(rows, PACKED_ROWS, norm1_apply, unroll=4)

    def in_proj(group):
        return _dot(xn_b[...], w_in_ref[:, group * D:(group + 1) * D])
    ux_s[p4:p4 + rows, :] = in_proj(0)
    uy_s[...] = in_proj(1)
    cv_s[...] = in_proj(2)
    cg_s[...] = in_proj(3)
    grnn_s[...] = in_proj(4)
    gconv_s[...] = in_proj(5)

    def conv4_glu(r0):
        acc = ux_s[pk(r0), :] * cw4_ref[0:1, :]
        for k in range(1, RNN_CONV_W):
            acc = acc + ux_s[pk(r0 + k * b_blk), :] * cw4_ref[k:k + 1, :]
        xc = acc + vec(V_CONV4_B)
        xc_s[pk(r0), :] = xc
        xc_b[pk(r0), :] = xc.astype(BF16)
        glu_s[pk(p31 + r0), :] = cv_s[pk(r0), :] * _sigmoid(cg_s[pk(r0), :])
    _row_loop(rows, PACKED_ROWS, conv4_glu, unroll=2)

    strip_t = min(SUBLANES, t_tile)
    n_strip_t = t_tile // strip_t
    assert n_strip_t * strip_t == t_tile

    def conv31_strip(q, carry):
        jb = q // n_strip_t
        tc = q - jb * n_strip_t
        r0 = pl.multiple_of(tc * (strip_t * b_blk) + jb * SUBLANES, SUBLANES)
        for col in range(N_COLS):
            cs = slice(col * LANES, (col + 1) * LANES)
            xs = [glu_s[pl.ds(r0 + j * b_blk, SUBLANES), cs] for j in range(strip_t + CONF_CONV_W - 1)]
            accs = [None] * strip_t
            for k in range(CONF_CONV_W):
                w = cw31_ref[k:k + 1, cs]
                for o in range(strip_t):
                    term = xs[o + k] * w
                    accs[o] = term if k == 0 else accs[o] + term
            bias = vec_ref[V_CONV31_B:V_CONV31_B + 1, cs]
            for o in range(strip_t):
                cv_s[pl.ds(r0 + o * b_blk, SUBLANES), cs] = accs[o] + bias
        return carry
    for_token_tiles(lambda: lax.fori_loop(0, (b_blk // SUBLANES) * n_strip_t, conv31_strip, 0))

    for blk in range(N_GATE_BLOCKS):
        cols = slice(blk * MXU_DIM, (blk + 1) * MXU_DIM)
        g = _dot(xc_b[:, cols], w_gate_ref[blk])
        ga_s[:, cols] = g[:, :MXU_DIM]
        gi_s[:, cols] = g[:, MXU_DIM:]

    c_lam = -LRU_C * jax.nn.softplus(-vec(V_LAMBDA))

    def lru_coeffs(r0):
        r = _sigmoid(ga_s[pk(r0), :] + vec(V_BA))
        ig = _sigmoid(gi_s[pk(r0), :] + vec(V_BI))
        log_a = c_lam * r
        a = jnp.exp(log_a)
        mult = jnp.sqrt(-jnp.tanh(log_a) * (a * a + 1.0))
        ga_s[pk(r0), :] = a
        gi_s[pk(r0), :] = mult * (ig * xc_s[pk(r0), :])
        uy_s[pk(r0), :] = jax.nn.gelu(uy_s[pk(r0), :])
    _row_loop(rows, PACKED_ROWS, lru_coeffs, unroll=2)

    if prefix_t:
        @pl.when(i == 0)
        def _():
            lead = (t_tile - prefix_t) * b_blk
            gi_s[0:lead, :] = jnp.zeros((lead, D), F32)

    def scan_batch_slab(jb, carry):
        b0 = pl.multiple_of(jb * SUBLANES, SUBLANES)

        def step(t, h):
            sl = pl.ds(pl.multiple_of(t * b_blk + b0, SUBLANES), SUBLANES)
            h = ga_s[sl, :] * h + gi_s[sl, :]
            hy_s[sl, :] = h * uy_s[sl, :]
            return h

        h_s[pl.ds(b0, SUBLANES), :] = lax.fori_loop(0, t_tile, step, h_s[pl.ds(b0, SUBLANES), :],
                                                    unroll=min(8, t_tile))
        return carry
    lax.fori_loop(0, b_blk // SUBLANES, scan_batch_slab, 0)

    def ln_stats(r0):
        cc = cv_s[pk(r0), :]
        mu = jnp.mean(cc, axis=-1, keepdims=True)
        cen = cc - mu
        var = jnp.mean(cen * cen, axis=-1, keepdims=True)
        st_s[pk(r0), :] = _on_lanes(mu)
        st2_s[pk(r0), :] = _on_lanes(lax.rsqrt(var + EPS))
    for_token_tiles(lambda: _row_loop(rows, PACKED_ROWS, ln_stats, unroll=STATS_UNROLL))

    def ln_silu(r0):
        cen = cv_s[pk(r0), :] - _across(st_s[pk(r0), :])
        y = cen * _across(st2_s[pk(r0), :]) * vec(V_LN_G) + vec(V_LN_B)
        cn_b[pk(r0), :] = (y * _sigmoid(y)).astype(BF16)
    for_token_tiles(lambda: _row_loop(rows, PACKED_ROWS, ln_silu, unroll=4))

    def token_outputs():
        ga_s[...] = _dot(hy_s[...].astype(BF16), w_rnn_ref[...])
        gi_s[...] = _dot(cn_b[...], w_conv_ref[...])

        def merge(r0):
            xn_b[pk(r0), :] = (_sigmoid(grnn_s[pk(r0), :]) * ga_s[pk(r0), :]
                               + _sigmoid(gconv_s[pk(r0), :]) * gi_s[pk(r0), :]).astype(BF16)
        _row_loop(rows, PACKED_ROWS, merge, unroll=2)
        cg_s[...] = _dot(xn_b[...], w_out_ref[...])

        def residual_stats(r0):
            x1 = x_s[pk(r0), :] + cg_s[pk(r0), :]
            x1c_ref[pk(r0), 0:D] = x1
            ms = jnp.mean(x1 * x1, axis=-1, keepdims=True)
            st_s[pk(r0), :] = _on_lanes(lax.rsqrt(ms + EPS))
        _row_loop(rows, PACKED_ROWS, residual_stats, unroll=STATS_UNROLL)

        def norm2_split(r0):
            xn2 = x1c_ref[pk(r0), 0:D] * _across(st_s[pk(r0), :]) * vec(V_NORM2)
            hi = xn2.astype(BF16)
            cn_b[pk(r0), :] = hi
            xc_b[pk(r0), :] = (xn2 - hi.astype(F32)).astype(BF16)
        _row_loop(rows, PACKED_ROWS, norm2_split, unroll=4)

        hi_prod = _dot(cn_b[...], wr_ref[...])
        logits = hi_prod[:, :LANES] + hi_prod[:, LANES:] + _dot(xc_b[...], wr_ref[:, :LANES]) + br_ref[...]
        x1c_ref[:, D:D + LANES] = _route_info(logits)
    for_token_tiles(token_outputs)

    for k in range(RNN_CONV_W - 1):
        ux_s[k * b_blk:(k + 1) * b_blk, :] = ux_s[(t_tile + k) * b_blk:(t_tile + k + 1) * b_blk, :]
    for k in range(CONF_CONV_W - 1):
        glu_s[k * b_blk:(k + 1) * b_blk, :] = glu_s[(t_tile + k) * b_blk:(t_tile + k + 1) * b_blk, :]

    @pl.when(i == pl.num_programs(1) - 1)
    def _():
        h_out_ref[...] = h_s[...]
        c4_out_ref[...] = ux_s[0:p4, :]
        c31_out_ref[...] = glu_s[0:p31, :]


def _mixer_call(x, h0, c4, c31, wts, *, t_tile, b_blk, batch_major=False, prefix_tokens=None):
    if batch_major:
        assert x.shape[0] == b_blk and PACKED_ROWS % b_blk == 0
        n_b, total_rows = 1, x.shape[0] * x.shape[1]
        tile_shape = (b_blk, t_tile, D)
    else:
        n_b, total_rows, _ = x.shape
        tile_shape = (t_tile * b_blk, D)
    rows = t_tile * b_blk
    n_t = total_rows // rows
    assert n_t * rows == total_rows and b_blk % SUBLANES == 0 and rows % PACKED_ROWS == 0
    assert t_tile >= RNN_CONV_W - 1
    p4 = (RNN_CONV_W - 1) * b_blk
    p31 = (CONF_CONV_W - 1) * b_blk
    if prefix_tokens is None:
        prefix_t, n_pre = 0, 0
        prefix = jnp.zeros(tile_shape, F32)
    else:
        assert batch_major and n_b == 1
        prefix_t, n_pre = prefix_tokens.shape[0], 1
        lead = jnp.zeros((b_blk, t_tile - prefix_t, D), F32)
        prefix = jnp.concatenate([lead, jnp.broadcast_to(prefix_tokens[None], (b_blk, prefix_t, D))], axis=1)

    def x_step(i):
        return jnp.maximum(i - n_pre, 0)

    def tile_spec(width):
        return pl.BlockSpec((None, rows, width), lambda j, i: (j, x_step(i), 0))

    def state_spec(n_rows):
        return pl.BlockSpec((None, n_rows, D), lambda j, i: (j, 0, 0))

    def const_spec(arr):
        nd = arr.ndim
        return pl.BlockSpec(arr.shape, lambda j, i: (0,) * nd, pipeline_mode=pl.Buffered(1))

    plane = pltpu.VMEM((rows, D), F32)
    plane_b = pltpu.VMEM((rows, D), BF16)
    return pl.pallas_call(
        functools.partial(_mixer_kernel, t_tile=t_tile, b_blk=b_blk, batch_major=batch_major, prefix_t=prefix_t),
        grid=(n_b, n_t + n_pre),
        in_specs=[pl.BlockSpec((b_blk, t_tile, D), lambda j, i: (0, x_step(i), 0)) if batch_major
                  else tile_spec(D),
                  const_spec(prefix), state_spec(b_blk), state_spec(p4), state_spec(p31)]
                 + [const_spec(w) for w in wts],
        out_specs=[pl.BlockSpec((rows, TOKEN_W), lambda j, i: (j * n_t + x_step(i), 0)),
                   state_spec(b_blk), state_spec(p4), state_spec(p31)],
        out_shape=[jax.ShapeDtypeStruct((n_b * total_rows, TOKEN_W), F32),
                   jax.ShapeDtypeStruct((n_b, b_blk, D), F32),
                   jax.ShapeDtypeStruct((n_b, p4, D), F32),
                   jax.ShapeDtypeStruct((n_b, p31, D), F32)],
        scratch_shapes=[plane_b, pltpu.VMEM((p4 + rows, D), F32), plane, plane, plane, plane, plane,
                        pltpu.VMEM((p31 + rows, D), F32), plane, plane_b, plane, plane, plane, plane_b,
                        pltpu.VMEM((b_blk, D), F32), plane, pltpu.VMEM((rows, LANES), F32),
                        pltpu.VMEM((rows, LANES), F32)],
        compiler_params=pltpu.CompilerParams(dimension_semantics=("arbitrary", "arbitrary"),
                                             vmem_limit_bytes=VMEM_LIMIT),
        name="mixer",
    )(x, prefix, h0, c4, c31, *wts)


def _row_copies(n, make_copy):
    def issue(r, carry):
        make_copy(r).start()
        return carry
    lax.fori_loop(0, n, issue, 0, unroll=8)


def _copy_tile(n_rows):
    tile = COPY_TILE
    while n_rows % tile:
        tile //= 2
    return tile


def _dispatch_kernel(slot_ref, pad0_ref, padn_ref, tok_a_ref, tok_b_ref, sorted_hbm, zero_s, sem, pad_sem,
                     *, n_a_tiles):
    i = pl.program_id(0)
    tile_a, tile_b = tok_a_ref.shape[0], tok_b_ref.shape[0]

    def scatter(tok_ref, token0):
        n = tok_ref.shape[0]

        def row_copy(r):
            return pltpu.make_async_copy(tok_ref.at[pl.ds(r, 1), :],
                                         sorted_hbm.at[pl.ds(slot_ref[token0 + r], 1), :], sem)
        _row_copies(n, row_copy)
        pltpu.make_async_copy(tok_ref, sorted_hbm.at[pl.ds(0, n), :], sem).wait()

    @pl.when(i < n_a_tiles)
    def _():
        scatter(tok_a_ref, i * tile_a)

    @pl.when(i >= n_a_tiles)
    def _():
        scatter(tok_b_ref, n_a_tiles * tile_a + (i - n_a_tiles) * tile_b)

    @pl.when(i == pl.num_programs(0) - 1)
    def _():
        zero_s[...] = jnp.zeros_like(zero_s)
        for region in range(N_PAD_REGIONS):
            def pad_copy(k, region=region):
                return pltpu.make_async_copy(zero_s.at[pl.ds(0, 1), :],
                                             sorted_hbm.at[pl.ds(pad0_ref[region] + k, 1), :], pad_sem)

            def issue(k, carry):
                pad_copy(k).start()
                return carry

            def drain(k, carry):
                pad_copy(k).wait()
                return carry
            lax.fori_loop(0, padn_ref[region], issue, 0)
            lax.fori_loop(0, padn_ref[region], drain, 0)


def _dispatch_call(tokens_a, tokens_b, slot, pad0, padn, *, n_sorted):
    n_a, n_b = tokens_a.shape[0], tokens_b.shape[0]
    tile_a, tile_b = _copy_tile(n_a), _copy_tile(n_b)
    n_a_tiles = n_a // tile_a
    return pl.pallas_call(
        functools.partial(_dispatch_kernel, n_a_tiles=n_a_tiles),
        grid_spec=pltpu.PrefetchScalarGridSpec(
            num_scalar_prefetch=3, grid=(n_a_tiles + n_b // tile_b,),
            in_specs=[pl.BlockSpec((tile_a, TOKEN_W), lambda i, *_: (jnp.minimum(i, n_a_tiles - 1), 0)),
                      pl.BlockSpec((tile_b, TOKEN_W), lambda i, *_: (jnp.maximum(i - n_a_tiles, 0), 0))],
            out_specs=pl.BlockSpec(memory_space=pl.ANY),
            scratch_shapes=[pltpu.VMEM((SUBLANES, TOKEN_W), F32), pltpu.SemaphoreType.DMA(()),
                            pltpu.SemaphoreType.DMA(())]),
        out_shape=jax.ShapeDtypeStruct((n_sorted, TOKEN_W), F32),
        compiler_params=pltpu.CompilerParams(dimension_semantics=("arbitrary",), vmem_limit_bytes=VMEM_LIMIT),
        name="moe_dispatch",
    )(slot, pad0, padn, tokens_a, tokens_b)


def _moe_kernel(tile_group_ref, n_tiles_ref, tok_ref, wg_ref, wu_ref, wd_ref, gains_ref, y_ref,
                xn_b, he_b, y_s, st_s):
    i = pl.program_id(0)
    live = i < n_tiles_ref[0]

    def pk(r0):
        return pl.ds(r0, PACKED_ROWS)

    @pl.when(live)
    def _():
        def norm2_stats(r0):
            x1 = tok_ref[pk(r0), 0:D]
            ms = jnp.mean(x1 * x1, axis=-1, keepdims=True)
            st_s[pk(r0), :] = _on_lanes(lax.rsqrt(ms + EPS))
        _row_loop(MOE_TILE, PACKED_ROWS, norm2_stats, unroll=STATS_UNROLL)

        def norm2(r0):
            xn_b[pk(r0), :] = (tok_ref[pk(r0), 0:D] * _across(st_s[pk(r0), :]) * gains_ref[0:1, :]).astype(BF16)
        _row_loop(MOE_TILE, PACKED_ROWS, norm2, unroll=4)

        info = tok_ref[:, D:D + LANES]
        lane = lax.broadcasted_iota(jnp.int32, info.shape, 1)
        lane0 = EXPERT_LANE0 + tile_group_ref[i] * EXPERTS_PER_GROUP
        for e in range(EXPERTS_PER_GROUP):
            scale = jnp.sum(jnp.where(lane == lane0 + e, info, 0.0), axis=-1, keepdims=True)
            gate = _dot(xn_b[...], wg_ref[e])
            he = gate * _sigmoid(gate) * _dot(xn_b[...], wu_ref[e])
            he_b[:, e * D_EXPERT:(e + 1) * D_EXPERT] = (he * scale).astype(BF16)
        y_s[...] = _dot(he_b[...], wd_ref[...])

        def residual_stats(r0):
            y = tok_ref[pk(r0), 0:D] + y_s[pk(r0), :]
            y_s[pk(r0), :] = y
            ms = jnp.mean(y * y, axis=-1, keepdims=True)
            st_s[pk(r0), :] = _on_lanes(lax.rsqrt(ms + EPS))
        _row_loop(MOE_TILE, PACKED_ROWS, residual_stats, unroll=STATS_UNROLL)

        def final_norm(r0):
            y_ref[pk(r0), :] = y_s[pk(r0), :] * _across(st_s[pk(r0), :]) * gains_ref[1:2, :]
        _row_loop(MOE_TILE, PACKED_ROWS, final_norm, unroll=4)

    @pl.when(jnp.logical_not(live))
    def _():
        y_ref[...] = jnp.zeros_like(y_ref)


def _moe_call(sorted_tokens, tile_group, n_tiles, wg, wu, wd, gains):
    n_sorted = sorted_tokens.shape[0]
    assert n_sorted % MOE_TILE == 0
    hidden = EXPERTS_PER_GROUP * D_EXPERT

    def tile(i, tile_group_ref, n_tiles_ref):
        return (jnp.minimum(i, n_tiles_ref[0] - 1), 0)

    def group(i, tile_group_ref, n_tiles_ref):
        return (tile_group_ref[jnp.minimum(i, n_tiles_ref[0] - 1)], 0, 0)

    return pl.pallas_call(
        _moe_kernel,
        grid_spec=pltpu.PrefetchScalarGridSpec(
            num_scalar_prefetch=2, grid=(n_sorted // MOE_TILE,),
            in_specs=[pl.BlockSpec((MOE_TILE, TOKEN_W), tile),
                      pl.BlockSpec((EXPERTS_PER_GROUP, D, D_EXPERT), group),
                      pl.BlockSpec((EXPERTS_PER_GROUP, D, D_EXPERT), group),
                      pl.BlockSpec((None, hidden, D), group),
                      pl.BlockSpec((SUBLANES, D), lambda i, *_: (0, 0))],
            out_specs=pl.BlockSpec((MOE_TILE, D), lambda i, *_: (i, 0)),
            scratch_shapes=[pltpu.VMEM((MOE_TILE, D), BF16), pltpu.VMEM((MOE_TILE, hidden), BF16),
                            pltpu.VMEM((MOE_TILE, D), F32), pltpu.VMEM((MOE_TILE, LANES), F32)]),
        out_shape=jax.ShapeDtypeStruct((n_sorted, D), F32),
        compiler_params=pltpu.CompilerParams(dimension_semantics=("arbitrary",),
                                             vmem_limit_bytes=VMEM_LIMIT),
        name="moe_grouped",
    )(tile_group, n_tiles, sorted_tokens, wg, wu, wd, gains)


def _collect_kernel(src_ref, y_hbm, o_ref, sem):
    i = pl.program_id(0)
    tile = o_ref.shape[0]

    def row_copy(r):
        return pltpu.make_async_copy(y_hbm.at[pl.ds(src_ref[i * tile + r], 1), :],
                                     o_ref.at[pl.ds(r, 1), :], sem)
    _row_copies(tile, row_copy)
    pltpu.make_async_copy(y_hbm.at[pl.ds(0, tile), :], o_ref, sem).wait()


def _collect_call(y_sorted, src):
    n = src.shape[0]
    tile = _copy_tile(n)
    return pl.pallas_call(
        _collect_kernel,
        grid_spec=pltpu.PrefetchScalarGridSpec(
            num_scalar_prefetch=1, grid=(n // tile,),
            in_specs=[pl.BlockSpec(memory_space=pl.ANY)],
            out_specs=pl.BlockSpec((tile, D), lambda i, *_: (i, 0)),
            scratch_shapes=[pltpu.SemaphoreType.DMA(())]),
        out_shape=jax.ShapeDtypeStruct((n, D), F32),
        compiler_params=pltpu.CompilerParams(dimension_semantics=("arbitrary",), vmem_limit_bytes=VMEM_LIMIT),
        name="moe_collect",
    )(src, y_sorted)


def _routing_tables(group, n_tile_cap):
    onehot = (group[:, None] == jnp.arange(N_GROUPS, dtype=jnp.int32)[None, :]).astype(jnp.int32)
    running = jnp.cumsum(onehot, axis=0)
    rank = jnp.sum(running * onehot, axis=1) - 1
    counts = running[-1]
    tiles_per_group = (counts + MOE_TILE - 1) // MOE_TILE
    end_tile = jnp.cumsum(tiles_per_group)
    first_row = (end_tile - tiles_per_group) * MOE_TILE
    slot = jnp.sum(first_row[None, :] * onehot, axis=1) + rank
    tile_ids = jnp.arange(n_tile_cap, dtype=jnp.int32)
    tile_group = jnp.minimum(jnp.sum((tile_ids[:, None] >= end_tile[None, :]).astype(jnp.int32), axis=1),
                             N_GROUPS - 1)
    used_rows = end_tile[-1:] * MOE_TILE
    pad0 = jnp.concatenate([first_row + counts, used_rows])
    padn = jnp.concatenate([tiles_per_group * MOE_TILE - counts, n_tile_cap * MOE_TILE - used_rows])
    return (slot.astype(jnp.int32), tile_group.astype(jnp.int32), end_tile[-1:].astype(jnp.int32),
            pad0.astype(jnp.int32), padn.astype(jnp.int32))


def _block_diag_gates(w_a, w_i):
    def bd(w):
        w = w.reshape(N_GATE_BLOCKS, HEADS_PER_BLOCK, HEAD_DIM, HEAD_DIM)
        eye = jnp.eye(HEADS_PER_BLOCK, dtype=w.dtype)
        full = jnp.einsum("bhij,hg->bhigj", w, eye)
        return full.reshape(N_GATE_BLOCKS, MXU_DIM, MXU_DIM)
    return jnp.concatenate([bd(w_a), bd(w_i)], axis=-1).astype(BF16)


def _to_time_major(a, n_b):
    b, t, d = a.shape
    return a.reshape(n_b, b // n_b, t, d).transpose(0, 2, 1, 3).reshape(n_b, t * (b // n_b), d)


def _from_time_major(a, t):
    n_b, rows, d = a.shape
    bb = rows // t
    return a.reshape(n_b, t, bb, d).transpose(0, 2, 1, 3).reshape(n_b * bb, t, d)


def kernel(x_prompt, x_sample, state_rglru_h, state_rglru_conv, state_conformer_conv, meta_tokens, norm1_g, w_in, rnn_conv_w, rnn_conv_b, w_rg_a, b_rg_a, w_rg_i, b_rg_i, rg_lambda, w_rnn_proj, conf_conv_w, conf_conv_b, conf_ln_g, conf_ln_b, w_conv_proj, w_out, norm2_g, w_group, b_group, w_erouter, b_erouter, w_exp_gate, w_exp_up, w_exp_down, final_norm_g):
    assert w_in.shape[0] == 1, "single-layer trunk"
    bsz, seq, _ = x_prompt.shape
    dec_b, dec_t, _ = x_sample.shape

    vec_rows = [norm1_g[0], rnn_conv_b[0], b_rg_a[0], b_rg_i[0], rg_lambda[0], conf_conv_b[0], conf_ln_g[0],
                conf_ln_b[0], norm2_g[0]]
    vec = jnp.zeros((N_VEC_ROWS, D), F32).at[:len(vec_rows)].set(jnp.stack(vec_rows).astype(F32))
    cw4 = jnp.zeros((SUBLANES, D), F32).at[:RNN_CONV_W].set(rnn_conv_w[0])
    cw31 = jnp.zeros((4 * SUBLANES, D), F32).at[:CONF_CONV_W].set(conf_conv_w[0])
    w_router = jnp.zeros((D, LANES), F32)
    w_router = w_router.at[:, :N_GROUPS].set(w_group[0]).at[:, N_GROUPS:N_GROUPS + N_EXPERTS].set(w_erouter[0])
    wr_hi = w_router.astype(BF16)
    wr_lo = (w_router - wr_hi.astype(F32)).astype(BF16)
    wr_split = jnp.concatenate([wr_hi, wr_lo], axis=1)
    b_router = jnp.zeros((1, LANES), F32)
    b_router = b_router.at[0, :N_GROUPS].set(b_group[0]).at[0, N_GROUPS:N_GROUPS + N_EXPERTS].set(
        b_erouter[0].reshape(-1))
    wts = (vec, cw4, cw31, w_in[0].astype(BF16), _block_diag_gates(w_rg_a[0], w_rg_i[0]),
           w_rnn_proj[0].astype(BF16), w_conv_proj[0].astype(BF16), w_out[0].astype(BF16), wr_split, b_router)
    wg = w_exp_gate[0].astype(BF16)
    wu = w_exp_up[0].astype(BF16)
    wd = w_exp_down[0].astype(BF16).reshape(N_GROUPS, EXPERTS_PER_GROUP * D_EXPERT, D)
    gains = jnp.zeros((SUBLANES, D), F32).at[0].set(norm2_g[0]).at[1].set(final_norm_g)

    n_p = bsz * seq
    n_s = dec_b * dec_t

    tok_p, h_p, c4_p, c31_p = _mixer_call(
        x_prompt, jnp.zeros((1, bsz, D), F32), jnp.zeros((1, (RNN_CONV_W - 1) * bsz, D), F32),
        jnp.zeros((1, (CONF_CONV_W - 1) * bsz, D), F32), wts, t_tile=PROMPT_T_TILE, b_blk=bsz, batch_major=True,
        prefix_tokens=meta_tokens.astype(F32))

    n_b = SAMPLE_BATCH_BLOCKS
    bb = dec_b // n_b
    tok_s, h_s, c4_s, c31_s = _mixer_call(
        _to_time_major(x_sample, n_b), state_rglru_h[0].reshape(n_b, bb, D),
        _to_time_major(state_rglru_conv[0], n_b), _to_time_major(state_conformer_conv[0], n_b), wts,
        t_tile=dec_t, b_blk=bb)

    n_tile_cap = (n_p + n_s) // MOE_TILE + N_GROUPS
    group = jnp.concatenate([tok_p[:, D], tok_s[:, D]]).astype(jnp.int32)
    slot, tile_group, n_tiles, pad0, padn = _routing_tables(group, n_tile_cap)
    sorted_tokens = _dispatch_call(tok_p, tok_s, slot, pad0, padn, n_sorted=n_tile_cap * MOE_TILE)
    y_sorted = _moe_call(sorted_tokens, tile_group, n_tiles, wg, wu, wd, gains)
    src_p = slot[:n_p].reshape(seq, bsz).T.reshape(-1)
    src_s = slot[n_p:].reshape(n_b, dec_t, bb).transpose(0, 2, 1).reshape(-1)
    y_prompt = _collect_call(y_sorted, src_p).reshape(bsz, seq, D)
    y_sample = _collect_call(y_sorted, src_s).reshape(dec_b, dec_t, D)

    return (y_prompt, y_sample,
            h_p.reshape(1, bsz, D), _from_time_major(c4_p, RNN_CONV_W - 1)[None],
            _from_time_major(c31_p, CONF_CONV_W - 1)[None],
            h_s.reshape(1, dec_b, D), _from_time_major(c4_s, RNN_CONV_W - 1)[None],
            _from_time_major(c31_s, CONF_CONV_W - 1)[None])
```
